```python
import math
import jax, jax.numpy as jnp
from jax import lax
import numpy as np

D_MODEL = 4096
BATCH = 4
SEQ = 4096
DEPTH = 2

HEAD_DIM = 128
POOL_WINDOWS = (2, 4, 8, 16)
POOL_GROUPS = 4
POOL_GROUP_DIM = D_MODEL // 16
POOL_DIM = POOL_GROUPS * POOL_GROUP_DIM
ATTN_DIM = (D_MODEL - POOL_DIM) // 2
FOX_HEADS = ATTN_DIM // HEAD_DIM
GDN_DIM = D_MODEL - POOL_DIM - ATTN_DIM
GDN_HEADS = GDN_DIM // HEAD_DIM
MIX_DIM = POOL_DIM + ATTN_DIM + GDN_DIM
Q_BLOCK = 128
GDN_CHUNK = 64
GDN_CONV = 4
FFN_DIM = 11008
FFN_CONV = 3
EPS = 1e-6
IN_SIZES = (POOL_DIM, 3 * ATTN_DIM, FOX_HEADS, 3 * GDN_DIM, GDN_DIM, GDN_HEADS, GDN_HEADS)
IN_DIM = sum(IN_SIZES)

kernel_name = "hybrid_pool_fox_gdn_convffn"


def rms_norm(x, gain):
    xf = x.astype(jnp.float32)
    xf = xf * lax.rsqrt(jnp.mean(xf * xf, axis=-1, keepdims=True) + EPS)
    return (xf * gain.astype(jnp.float32)).astype(x.dtype)


def l2_normalize(x):
    return x * lax.rsqrt(jnp.sum(x * x, axis=-1, keepdims=True) + EPS)


def causal_depthwise_conv(x, w):
    k_width = w.shape[0]
    t = x.shape[1]
    xp = jnp.pad(x, ((0, 0), (k_width - 1, 0), (0, 0)))
    return sum(xp[:, i:i + t] * w[i] for i in range(k_width))


def split_columns(h):
    outs = []
    start = 0
    for size in IN_SIZES:
        outs.append(h[..., start:start + size])
        start += size
    return outs


def multiscale_pool_mixer(v, w_group, scale):
    b, t, _ = v.shape
    vf = v.astype(jnp.float32).reshape(b, t, POOL_GROUPS, POOL_GROUP_DIM)
    cs = jnp.cumsum(vf, axis=1)
    pos = jnp.arange(1, t + 1, dtype=jnp.float32)
    outs = []
    for gi, win in enumerate(POOL_WINDOWS):
        c = cs[:, :, gi]
        shifted = jnp.pad(c, ((0, 0), (win, 0), (0, 0)))[:, :t]
        mean = (c - shifted) / jnp.minimum(pos, float(win))[None, :, None]
        outs.append(mean - vf[:, :, gi])
    pooled = jnp.stack(outs, axis=2).astype(v.dtype)
    y = jnp.einsum('btgc,gce->btge', pooled, w_group).reshape(b, t, POOL_DIM)
    return y * scale


def forgetting_attention(q, k, v, log_f):
    b, t, h, d = q.shape
    scale = 1.0 / math.sqrt(d)
    cum = jnp.transpose(jnp.cumsum(log_f, axis=1), (0, 2, 1))
    outs = []
    for blk in range(t // Q_BLOCK):
        t0, t1 = blk * Q_BLOCK, (blk + 1) * Q_BLOCK
        qb, kb, vb = q[:, t0:t1], k[:, :t1], v[:, :t1]
        s = jnp.einsum('bqhd,bkhd->bhqk', qb, kb).astype(jnp.float32) * scale
        s = s + (cum[:, :, t0:t1, None] - cum[:, :, None, :t1])
        mask = jnp.arange(t0, t1)[:, None] >= jnp.arange(t1)[None, :]
        s = jnp.where(mask, s, -jnp.inf)
        p = jax.nn.softmax(s, axis=-1).astype(v.dtype)
        outs.append(jnp.einsum('bhqk,bkhd->bqhd', p, vb))
    return jnp.concatenate(outs, axis=1)


def gated_delta_rule(q, k, v, g, beta):
    b, t, h, d = q.shape
    n = t // GDN_CHUNK
    q = l2_normalize(q) * (d ** -0.5)
    k = l2_normalize(k)

    def chunks(a):
        a = a.reshape((b, n, GDN_CHUNK, h) + a.shape[3:])
        return jnp.moveaxis(a, 3, 1)

    q, k, v = chunks(q), chunks(k), chunks(v)
    beta = chunks(beta)
    g = jnp.cumsum(chunks(g), axis=-1)
    idx = jnp.arange(GDN_CHUNK)
    causal = idx[:, None] >= idx[None, :]
    strict = idx[:, None] > idx[None, :]
    decay = jnp.exp(jnp.where(causal, g[..., :, None] - g[..., None, :], -jnp.inf))
    k_beta = k * beta[..., None]
    a_mat = jnp.where(strict, jnp.einsum('bhncd,bhnsd->bhncs', k_beta, k) * decay, 0.0)
    rhs = jnp.concatenate([v * beta[..., None], k_beta * jnp.exp(g)[..., None]], axis=-1)
    eye = jnp.eye(GDN_CHUNK, dtype=jnp.float32)
    sol = lax.linalg.triangular_solve(a_mat + eye, rhs, left_side=True, lower=True,
                                      unit_diagonal=True)
    u, w = sol[..., :d], sol[..., d:]
    intra = jnp.where(causal, jnp.einsum('bhncd,bhnsd->bhncs', q, k) * decay, 0.0)
    g_last = g[..., -1]
    k_dec = k * jnp.exp(g_last[..., None] - g)[..., None]
    q_dec = q * jnp.exp(g)[..., None]

    def step(state, inp):
        q_n, k_n, u_n, w_n, intra_n, gl_n = inp
        v_new = u_n - jnp.einsum('bhcd,bhde->bhce', w_n, state)
        o = jnp.einsum('bhcd,bhde->bhce', q_n, state) + jnp.einsum('bhcs,bhse->bhce', intra_n, v_new)
        state = state * jnp.exp(gl_n)[..., None, None] + jnp.einsum('bhcd,bhce->bhde', k_n, v_new)
        return state, o

    xs = tuple(jnp.moveaxis(a, 2, 0) for a in (q_dec, k_dec, u, w, intra, g_last))
    state0 = jnp.zeros((b, h, d, v.shape[-1]), jnp.float32)
    _, o = lax.scan(step, state0, xs)
    return jnp.transpose(o, (1, 0, 3, 2, 4)).reshape(b, t, h, -1)


def setup_inputs(seed: int = 0) -> dict:
    key = jax.random.key(seed)
    ks = jax.random.split(key, 16)
    f32 = jnp.float32
    x = jax.random.normal(ks[0], (BATCH, SEQ, D_MODEL), f32)
    norm_mix_gain = 1.0 + 0.05 * jax.random.normal(ks[1], (DEPTH, D_MODEL), f32)
    w_in = jax.random.normal(ks[2], (DEPTH, D_MODEL, IN_DIM), f32) * D_MODEL ** -0.5
    pool_w = jax.random.normal(ks[3], (DEPTH, POOL_GROUPS, POOL_GROUP_DIM, POOL_GROUP_DIM), f32) * POOL_GROUP_DIM ** -0.5
    pool_scale = 1.0 + 0.1 * jax.random.normal(ks[4], (DEPTH, POOL_DIM), f32)
    fox_f_bias = 2.0 + 0.5 * jax.random.normal(ks[5], (DEPTH, FOX_HEADS), f32)
    gdn_conv_w = jax.random.normal(ks[6], (DEPTH, GDN_CONV, 3 * GDN_DIM), f32) * GDN_CONV ** -0.5
    gdn_A_log = jnp.log(jax.random.uniform(ks[7], (DEPTH, GDN_HEADS), f32, 1.0, 16.0))
    dt = jnp.exp(jax.random.uniform(ks[8], (DEPTH, GDN_HEADS), f32, math.log(1e-3), math.log(1e-1)))
    gdn_dt_bias = dt + jnp.log(-jnp.expm1(-dt))
    gdn_norm_gain = 1.0 + 0.05 * jax.random.normal(ks[9], (DEPTH, HEAD_DIM), f32)
    w_o = jax.random.normal(ks[10], (DEPTH, MIX_DIM, D_MODEL), f32) * MIX_DIM ** -0.5
    norm_ffn_gain = 1.0 + 0.05 * jax.random.normal(ks[11], (DEPTH, D_MODEL), f32)
    w_up = jax.random.normal(ks[12], (DEPTH, D_MODEL, 2 * FFN_DIM), f32) * D_MODEL ** -0.5
    ffn_conv_w = jax.random.normal(ks[13], (DEPTH, FFN_CONV, 2 * FFN_DIM), f32) * FFN_CONV ** -0.5
    w_down = jax.random.normal(ks[14], (DEPTH, FFN_DIM, D_MODEL), f32) * FFN_DIM ** -0.5
    final_norm_gain = 1.0 + 0.05 * jax.random.normal(ks[15], (D_MODEL,), f32)
    return {"x": x, "norm_mix_gain": norm_mix_gain, "w_in": w_in, "pool_w": pool_w,
            "pool_scale": pool_scale, "fox_f_bias": fox_f_bias, "gdn_conv_w": gdn_conv_w,
            "gdn_A_log": gdn_A_log, "gdn_dt_bias": gdn_dt_bias, "gdn_norm_gain": gdn_norm_gain,
            "w_o": w_o, "norm_ffn_gain": norm_ffn_gain, "w_up": w_up, "ffn_conv_w": ffn_conv_w,
            "w_down": w_down, "final_norm_gain": final_norm_gain}


def reference(x, norm_mix_gain, w_in, pool_w, pool_scale, fox_f_bias, gdn_conv_w, gdn_A_log,
              gdn_dt_bias, gdn_norm_gain, w_o, norm_ffn_gain, w_up, ffn_conv_w, w_down,
              final_norm_gain):
    b, t, _ = x.shape
    f32 = jnp.float32
    for l in range(DEPTH):
        h = rms_norm(x, norm_mix_gain[l])
        proj = h @ w_in[l]
        pool_in, fox_qkv, fox_f, gdn_qkv, gdn_z, gdn_b, gdn_a = split_columns(proj)

        y_pool = multiscale_pool_mixer(pool_in, pool_w[l], pool_scale[l])

        fq, fk, fv = (a.reshape(b, t, FOX_HEADS, HEAD_DIM) for a in jnp.split(fox_qkv, 3, axis=-1))
        log_f = jax.nn.log_sigmoid(fox_f.astype(f32) + fox_f_bias[l].astype(f32))
        y_fox = forgetting_attention(fq, fk, fv, log_f).reshape(b, t, ATTN_DIM)

        gqkv = jax.nn.silu(causal_depthwise_conv(gdn_qkv, gdn_conv_w[l]))
        gq, gk, gv = (a.reshape(b, t, GDN_HEADS, HEAD_DIM).astype(f32) for a in jnp.split(gqkv, 3, axis=-1))
        beta = jax.nn.sigmoid(gdn_b.astype(f32))
        g = -jnp.exp(gdn_A_log[l].astype(f32)) * jax.nn.softplus(gdn_a.astype(f32) + gdn_dt_bias[l].astype(f32))
        o = gated_delta_rule(gq, gk, gv, g, beta)
        z = gdn_z.reshape(b, t, GDN_HEADS, HEAD_DIM).astype(f32)
        o = rms_norm(o, gdn_norm_gain[l]) * jax.nn.silu(z)
        y_gdn = o.reshape(b, t, GDN_DIM).astype(x.dtype)

        mix = jnp.concatenate([y_pool.astype(x.dtype), y_fox.astype(x.dtype), y_gdn], axis=-1)
        x = x + mix @ w_o[l]

        h = rms_norm(x, norm_ffn_gain[l])
        u = causal_depthwise_conv(h @ w_up[l], ffn_conv_w[l])
        gate, up = jnp.split(u, 2, axis=-1)
        x = x + (jax.nn.silu(gate) * up) @ w_down[l]
    return rms_norm(x, final_norm_gain)
```

```python
import functools
import math

import jax
import jax.numpy as jnp
from jax import lax
from jax.experimental import pallas as pl
from jax.experimental.pallas import tpu as pltpu

D_MODEL = 4096
HEAD_DIM = 128
POOL_WINDOWS = (2, 4, 8, 16)
POOL_GROUPS = 4
POOL_GROUP_DIM = D_MODEL // 16
POOL_DIM = POOL_GROUPS * POOL_GROUP_DIM
ATTN_DIM = (D_MODEL - POOL_DIM) // 2
FOX_HEADS = ATTN_DIM // HEAD_DIM
GDN_DIM = D_MODEL - POOL_DIM - ATTN_DIM
GDN_HEADS = GDN_DIM // HEAD_DIM
GDN_CONV = 4
FFN_DIM = 11008
FFN_CONV = 3
EPS = 1e-6

LANES = 128
SUBLANES = 8
VMEM_BYTES_V7X = 64 * 1024 * 1024
VMEM_CAP = VMEM_BYTES_V7X - 8 * 1024 * 1024

BIG_DIM = POOL_DIM + 3 * ATTN_DIM + 3 * GDN_DIM + GDN_DIM
FOX_COL0 = POOL_DIM // LANES
GDN_COL0 = (POOL_DIM + 3 * ATTN_DIM) // LANES
Z_COL0 = (POOL_DIM + 3 * ATTN_DIM + 3 * GDN_DIM) // LANES
LANE_F = 0
LANE_BETA = FOX_HEADS
LANE_G = FOX_HEADS + GDN_HEADS
GATE_ROWS = 48
GDN_CHUNK = 128

F32 = jnp.float32
BF16 = jnp.bfloat16
NT_DIMS = (((1,), (1,)), ((), ()))
TN_DIMS = (((0,), (0,)), ((), ()))


def _params(semantics, vmem_estimate):
    limit = min(int(vmem_estimate * 1.25) + (4 << 20), VMEM_CAP)
    return pltpu.CompilerParams(dimension_semantics=semantics, vmem_limit_bytes=limit)


def _sigmoid(x):
    return 1.0 / (1.0 + jnp.exp(-x))


def _softplus(x):
    return jnp.maximum(x, 0.0) + jnp.log1p(jnp.exp(-jnp.abs(x)))


def _rms_kernel(x_ref, g_ref, o_ref):
    x = x_ref[...]
    ms = jnp.mean(x * x, axis=-1, keepdims=True)
    o_ref[...] = (x * lax.rsqrt(ms + EPS) * g_ref[...]).astype(o_ref.dtype)


def _rmsnorm(x, gain, out_dtype, tm=512):
    m, d = x.shape
    est = 2 * tm * d * (4 + jnp.dtype(out_dtype).itemsize)
    return pl.pallas_call(
        _rms_kernel,
        grid=(m // tm,),
        in_specs=[pl.BlockSpec((tm, d), lambda i: (i, 0)),
                  pl.BlockSpec((1, d), lambda i: (0, 0))],
        out_specs=pl.BlockSpec((tm, d), lambda i: (i, 0)),
        out_shape=jax.ShapeDtypeStruct((m, d), out_dtype),
        compiler_params=_params(("arbitrary",), est),
        name="rmsnorm",
    )(x, gain.reshape(1, d))


def _mm_kernel(*refs, n_a, has_res, k_steps):
    a_refs = refs[:n_a]
    b_refs = refs[n_a:2 * n_a]
    res_ref = refs[2 * n_a] if has_res else None
    o_ref = refs[-1]
    acc = None
    for a_ref, b_ref in zip(a_refs, b_refs):
        part = jnp.dot(a_ref[...], b_ref[...], preferred_element_type=F32)
        acc = part if acc is None else acc + part
    if k_steps == 1:
        if has_res:
            acc = acc + res_ref[...]
        o_ref[...] = acc.astype(o_ref.dtype)
    else:
        k = pl.program_id(2)

        @pl.when(k == 0)
        def _():
            o_ref[...] = (acc + res_ref[...]) if has_res else acc

        @pl.when(k > 0)
        def _():
            o_ref[...] += acc


def _matmul(a_list, b_list, out_dtype, tm, tn, residual=None, k_steps=1, name="matmul"):
    m = a_list[0].shape[0]
    n = b_list[0].shape[1]
    n_a = len(a_list)
    has_res = residual is not None
    assert k_steps == 1 or (n_a == 1 and out_dtype == F32)
    in_specs, est = [], 0
    for a in a_list:
        tk = a.shape[1] // k_steps
        in_specs.append(pl.BlockSpec((tm, tk), lambda i, j, k: (i, k)))
        est += 2 * tm * tk * a.dtype.itemsize
    for b in b_list:
        tk = b.shape[0] // k_steps
        in_specs.append(pl.BlockSpec((tk, tn), lambda i, j, k: (k, j)))
        est += 2 * tk * tn * b.dtype.itemsize
    args = list(a_list) + list(b_list)
    if has_res:
        in_specs.append(pl.BlockSpec((tm, tn), lambda i, j, k: (i, j)))
        est += 2 * tm * tn * 4
        args.append(residual)
    est += 2 * tm * tn * jnp.dtype(out_dtype).itemsize + 2 * tm * tn * 4
    return pl.pallas_call(
        functools.partial(_mm_kernel, n_a=n_a, has_res=has_res, k_steps=k_steps),
        grid=(m // tm, n // tn, k_steps),
        in_specs=in_specs,
        out_specs=pl.BlockSpec((tm, tn), lambda i, j, k: (i, j)),
        out_shape=jax.ShapeDtypeStruct((m, n), out_dtype),
        compiler_params=_params(("arbitrary", "arbitrary", "arbitrary"), est),
        name=name,
    )(*args)


def _scan_rows(y, row, seg):
    pos = row & (seg - 1)
    s = 1
    while s < seg:
        y = y + jnp.where(pos >= s, pltpu.roll(y, s, axis=0), 0.0)
        s *= 2
    return y


def _gates_kernel(x_ref, p_ref, g_ref, gt_ref, carry_ref):
    t = pl.program_id(1)

    @pl.when(t == 0)
    def _():
        carry_ref[...] = jnp.zeros_like(carry_ref)

    tb = x_ref.shape[0]
    z = x_ref[...] + p_ref[0:1, :]
    lane = lax.broadcasted_iota(jnp.int32, z.shape, 1)
    row = lax.broadcasted_iota(jnp.int32, z.shape, 0)
    log_f = -_softplus(-z)
    beta = _sigmoid(z)
    g = -jnp.exp(p_ref[1:2, :]) * _softplus(z)
    cum_f = _scan_rows(log_f, row, tb) + carry_ref[0:1, :]
    carry_ref[0:1, :] = cum_f[tb - 1:tb, :]
    cum_g = _scan_rows(g, row, GDN_CHUNK)
    out = jnp.where(lane < LANE_BETA, cum_f, jnp.where(lane < LANE_G, beta, cum_g))
    g_ref[...] = out
    gt_ref[...] = out.T[:GATE_ROWS, :]


def _gates(small, bias_row, alog_row, batch, seq, tb=512):
    m = small.shape[0]
    nt = seq // tb
    params = jnp.zeros((SUBLANES, LANES), F32).at[0].set(bias_row).at[1].set(alog_row)
    est = 2 * tb * LANES * 4 * 3 + 16 * tb * LANES * 4
    return pl.pallas_call(
        _gates_kernel,
        grid=(batch, nt),
        in_specs=[pl.BlockSpec((tb, LANES), lambda b, t: (b * nt + t, 0)),
                  pl.BlockSpec((SUBLANES, LANES), lambda b, t: (0, 0))],
        out_specs=[pl.BlockSpec((tb, LANES), lambda b, t: (b * nt + t, 0)),
                   pl.BlockSpec((GATE_ROWS, tb), lambda b, t: (0, b * nt + t))],
        out_shape=[jax.ShapeDtypeStruct((m, LANES), F32),
                   jax.ShapeDtypeStruct((GATE_ROWS, m), F32)],
        scratch_shapes=[pltpu.VMEM((SUBLANES, LANES), F32)],
        compiler_params=_params(("arbitrary", "arbitrary"), est),
        name="gates",
    )(small, params)


def _pool_kernel(x_ref, w_ref, sc_ref, o_ref, tail_ref):
    t = pl.program_id(1)

    @pl.when(t == 0)
    def _():
        tail_ref[...] = jnp.zeros_like(tail_ref)

    tb = x_ref.shape[0]
    halo = tail_ref.shape[0]
    cg = POOL_GROUP_DIM
    x = x_ref[...].astype(F32)
    xe = jnp.concatenate([tail_ref[...], x], axis=0)
    tail_ref[...] = x[tb - halo:, :]
    pos = (t * tb + 1 + lax.broadcasted_iota(jnp.int32, (tb, cg), 0)).astype(F32)
    for gi, win in enumerate(POOL_WINDOWS):
        s = xe[:, gi * cg:(gi + 1) * cg]
        span = 1
        while span < win:
            s = s + pltpu.roll(s, span, axis=0)
            span *= 2
        mean = s[halo:, :] / jnp.minimum(pos, float(win))
        pooled = (mean - x[:, gi * cg:(gi + 1) * cg]).astype(BF16)
        y = jnp.dot(pooled, w_ref[gi], preferred_element_type=F32)
        o_ref[:, gi * cg:(gi + 1) * cg] = (y * sc_ref[:, gi * cg:(gi + 1) * cg]).astype(o_ref.dtype)


def _pool(big, pool_w, pool_scale, batch, seq, tb=512):
    m = big.shape[0]
    nt = seq // tb
    halo = 16
    assert halo >= max(POOL_WINDOWS)
    est = 2 * tb * POOL_DIM * 4 + 8 * tb * POOL_DIM * 4
    return pl.pallas_call(
        _pool_kernel,
        grid=(batch, nt),
        in_specs=[pl.BlockSpec((tb, POOL_DIM), lambda b, t: (b * nt + t, 0)),
                  pl.BlockSpec((POOL_GROUPS, POOL_GROUP_DIM, POOL_GROUP_DIM), lambda b, t: (0, 0, 0)),
                  pl.BlockSpec((1, POOL_DIM), lambda b, t: (0, 0))],
        out_specs=pl.BlockSpec((tb, POOL_DIM), lambda b, t: (b * nt + t, 0)),
        out_shape=jax.ShapeDtypeStruct((m, POOL_DIM), BF16),
        scratch_shapes=[pltpu.VMEM((halo, POOL_DIM), F32)],
        compiler_params=_params(("arbitrary", "arbitrary"), est),
        name="pool_mixer",
    )(big, pool_w, pool_scale.reshape(1, POOL_DIM))


def _fox_kernel(q_ref, k_ref, v_ref, g_ref, cr_ref, o_ref, *, tq):
    h = pl.program_id(1)
    qi = pl.program_id(2)
    scale = 1.0 / math.sqrt(HEAD_DIM)
    q = q_ref[...]
    gates = g_ref[...]
    lane = lax.broadcasted_iota(jnp.int32, gates.shape, 1)
    cq = jnp.sum(jnp.where(lane == LANE_F + h, gates, 0.0), axis=1, keepdims=True)
    tri = (lax.broadcasted_iota(jnp.int32, (tq, tq), 0) >= lax.broadcasted_iota(jnp.int32, (tq, tq), 1))

    def step(ki, carry, masked):
        m_prev, l_prev, acc = carry
        start = pl.multiple_of(ki * tq, tq)
        kb = k_ref[pl.ds(start, tq), :]
        vb = v_ref[pl.ds(start, tq), :]
        s = lax.dot_general(q, kb, NT_DIMS, preferred_element_type=F32)
        zp = s * scale - cr_ref[ki]
        if masked:
            zp = jnp.where(tri, zp, -jnp.inf)
        m_new = jnp.maximum(m_prev, jnp.max(zp, axis=1, keepdims=True) + cq)
        p = jnp.exp(zp - (m_new - cq))
        alpha = jnp.exp(m_prev - m_new)
        l_new = alpha * l_prev + jnp.sum(p, axis=1, keepdims=True)
        acc = alpha * acc + jnp.dot(p.astype(BF16), vb, preferred_element_type=F32)
        return m_new, l_new, acc

    init = (jnp.full((tq, 1), -jnp.inf, F32), jnp.zeros((tq, 1), F32), jnp.zeros((tq, HEAD_DIM), F32))
    carry = lax.fori_loop(0, qi, lambda ki, c: step(ki, c, False), init)
    _, l_fin, acc = step(qi, carry, True)
    o_ref[...] = (acc / l_fin).astype(o_ref.dtype)


def _fox_attention(big, gates, gates_t, batch, seq, tq=512):
    m = big.shape[0]
    nq = seq // tq
    cr = gates_t.reshape(GATE_ROWS, m // tq, 1, tq)
    est = 2 * (tq * HEAD_DIM * 2 * 2 + 2 * seq * HEAD_DIM * 2 + tq * LANES * 4 + seq * 4 * 8) + 10 * tq * tq * 4
    return pl.pallas_call(
        functools.partial(_fox_kernel, tq=tq),
        grid=(batch, FOX_HEADS, nq),
        in_specs=[
            pl.BlockSpec((tq, HEAD_DIM), lambda b, h, i: (b * nq + i, FOX_COL0 + h)),
            pl.BlockSpec((seq, HEAD_DIM), lambda b, h, i: (b, FOX_COL0 + FOX_HEADS + h)),
            pl.BlockSpec((seq, HEAD_DIM), lambda b, h, i: (b, FOX_COL0 + 2 * FOX_HEADS + h)),
            pl.BlockSpec((tq, LANES), lambda b, h, i: (b * nq + i, 0)),
            pl.BlockSpec((None, nq, 1, tq), lambda b, h, i: (LANE_F + h, b, 0, 0)),
        ],
        out_specs=pl.BlockSpec((tq, HEAD_DIM), lambda b, h, i: (b * nq + i, h)),
        out_shape=jax.ShapeDtypeStruct((m, ATTN_DIM), BF16),
        compiler_params=_params(("arbitrary", "arbitrary", "arbitrary"), est),
        name="fox_attention",
    )(big, big, big, gates, cr)


def _hdot(a, b):
    return jnp.dot(a, b, preferred_element_type=F32, precision=lax.Precision.HIGHEST)


def _inv_unit_lower(low, row, col):
    n = low.shape[0]
    eye = (row == col).astype(F32)
    d = jnp.where((row >> 3) == (col >> 3), low, 0.0)
    d2 = _hdot(d, d)
    d4 = _hdot(d2, d2)
    inv = eye - d
    inv = inv + _hdot(inv, d2)
    inv = inv + _hdot(inv, d4)
    shift = 3
    while (1 << shift) < n:
        rb = row >> shift
        cb = col >> shift
        join = ((rb & 1) == 1) & (cb == rb - 1)
        x = _hdot(jnp.where(join, low, 0.0), inv)
        inv = inv - _hdot(inv, x)
        shift += 1
    return inv


def _gdn_kernel(q_ref, k_ref, v_ref, z_ref, g_ref, gr_ref, cwq_ref, cwk_ref, cwv_ref, gain_ref,
                o_ref, state_ref, tail_ref):
    h = pl.program_id(1)
    t = pl.program_id(2)

    @pl.when(t == 0)
    def _():
        state_ref[...] = jnp.zeros_like(state_ref)
        tail_ref[...] = jnp.zeros_like(tail_ref)

    tb = q_ref.shape[0]
    c = GDN_CHUNK
    dk = HEAD_DIM

    def conv_silu(x_ref, w_ref, slot):
        x = x_ref[...].astype(F32)
        xe = jnp.concatenate([tail_ref[slot], x], axis=0)
        w = w_ref[...]
        y = x * w[GDN_CONV - 1:GDN_CONV, :]
        for back in range(1, GDN_CONV):
            y = y + pltpu.roll(xe, back, axis=0)[SUBLANES:, :] * w[GDN_CONV - 1 - back:GDN_CONV - back, :]
        tail_ref[slot] = x[tb - SUBLANES:, :]
        return y * _sigmoid(y)

    def l2n(x):
        return x * lax.rsqrt(jnp.sum(x * x, axis=-1, keepdims=True) + EPS)

    q_all = l2n(conv_silu(q_ref, cwq_ref, 0)) * (dk ** -0.5)
    k_all = l2n(conv_silu(k_ref, cwk_ref, 1))
    v_all = conv_silu(v_ref, cwv_ref, 2)

    gates = g_ref[...]
    lane = lax.broadcasted_iota(jnp.int32, gates.shape, 1)
    beta_all = jnp.sum(jnp.where(lane == LANE_BETA + h, gates, 0.0), axis=1, keepdims=True)
    gcol_all = jnp.sum(jnp.where(lane == LANE_G + h, gates, 0.0), axis=1, keepdims=True)
    grow_all = gr_ref[...]

    row = lax.broadcasted_iota(jnp.int32, (c, c), 0)
    col = lax.broadcasted_iota(jnp.int32, (c, c), 1)
    causal = row >= col
    strict = row > col
    last_lane = lax.broadcasted_iota(jnp.int32, (1, c), 1) == c - 1

    state = state_ref[...]
    for ci in range(tb // c):
        sl = slice(ci * c, (ci + 1) * c)
        q, k, v = q_all[sl], k_all[sl], v_all[sl]
        beta, gcol, grow = beta_all[sl], gcol_all[sl], grow_all[:, sl]
        g_last = jnp.sum(jnp.where(last_lane, grow, 0.0), axis=1, keepdims=True)
        decay = jnp.exp(jnp.where(causal, gcol - grow, -jnp.inf))
        k_beta = k * beta
        kb16, k16 = k_beta.astype(BF16), k.astype(BF16)
        a_mat = jnp.where(strict, lax.dot_general(kb16, k16, NT_DIMS, preferred_element_type=F32) * decay, 0.0)
        inv = _inv_unit_lower(a_mat, row, col)
        rhs = jnp.concatenate([v * beta, k_beta * jnp.exp(gcol)], axis=1)
        sol = _hdot(inv, rhs)
        u, w = sol[:, :dk], sol[:, dk:]
        intra = jnp.where(causal, lax.dot_general(q.astype(BF16), k16, NT_DIMS, preferred_element_type=F32) * decay, 0.0)
        k_dec = k * jnp.exp(g_last - gcol)
        q_dec = q * jnp.exp(gcol)
        wq = jnp.concatenate([w, q_dec], axis=0).astype(BF16)
        ws = jnp.dot(wq, state.astype(BF16), preferred_element_type=F32)
        v_new = u - ws[:c]
        vn16 = v_new.astype(BF16)
        o = ws[c:] + jnp.dot(intra.astype(BF16), vn16, preferred_element_type=F32)
        state = state * jnp.exp(g_last) + lax.dot_general(k_dec.astype(BF16), vn16, TN_DIMS,
                                                          preferred_element_type=F32)
        z = z_ref[sl, :].astype(F32)
        o = o * lax.rsqrt(jnp.mean(o * o, axis=-1, keepdims=True) + EPS) * gain_ref[...]
        o_ref[sl, :] = (o * (z * _sigmoid(z))).astype(o_ref.dtype)
    state_ref[...] = state


def _gdn(big, gates, gates_t, conv_w, norm_gain, batch, seq, tb=512):
    m = big.shape[0]
    nt = seq // tb
    gr = gates_t.reshape(GATE_ROWS, 1, m)
    est = 2 * (5 * tb * HEAD_DIM * 2 + tb * LANES * 4 + tb * 32) + 64 * tb * HEAD_DIM * 4
    blk = lambda off: pl.BlockSpec((tb, HEAD_DIM), lambda b, h, t: (b * nt + t, off + h))
    cw = lambda off: pl.BlockSpec((GDN_CONV, HEAD_DIM), lambda b, h, t: (0, off + h))
    return pl.pallas_call(
        _gdn_kernel,
        grid=(batch, GDN_HEADS, nt),
        in_specs=[
            blk(GDN_COL0), blk(GDN_COL0 + GDN_HEADS), blk(GDN_COL0 + 2 * GDN_HEADS), blk(Z_COL0),
            pl.BlockSpec((tb, LANES), lambda b, h, t: (b * nt + t, 0)),
            pl.BlockSpec((None, 1, tb), lambda b, h, t: (LANE_G + h, 0, b * nt + t)),
            cw(0), cw(GDN_HEADS), cw(2 * GDN_HEADS),
            pl.BlockSpec((1, HEAD_DIM), lambda b, h, t: (0, 0)),
        ],
        out_specs=pl.BlockSpec((tb, HEAD_DIM), lambda b, h, t: (b * nt + t, h)),
        out_shape=jax.ShapeDtypeStruct((m, GDN_DIM), BF16),
        scratch_shapes=[pltpu.VMEM((HEAD_DIM, HEAD_DIM), F32),
                        pltpu.VMEM((3, SUBLANES, HEAD_DIM), F32)],
        compiler_params=_params(("arbitrary", "arbitrary", "arbitrary"), est),
        name="gated_delta_rule",
    )(big, big, big, big, gates, gr, conv_w, conv_w, conv_w, norm_gain.reshape(1, HEAD_DIM))


def _ffn_up_kernel(h_ref, wg_ref, wu_ref, cg_ref, cu_ref, o_ref, tail_ref, *, blocks_per_seq):
    i = pl.program_id(0)
    j = pl.program_id(1)
    tm = h_ref.shape[0]

    @pl.when(i % blocks_per_seq == 0)
    def _():
        tail_ref[j] = jnp.zeros(tail_ref.shape[1:], F32)

    h = h_ref[...]
    yg = jnp.dot(h, wg_ref[...], preferred_element_type=F32)
    yu = jnp.dot(h, wu_ref[...], preferred_element_type=F32)
    tail = tail_ref[j]

    def conv(y, prev, w):
        ye = jnp.concatenate([prev, y], axis=0)
        out = y * w[FFN_CONV - 1:FFN_CONV, :]
        for back in range(1, FFN_CONV):
            out = out + pltpu.roll(ye, back, axis=0)[SUBLANES:, :] * w[FFN_CONV - 1 - back:FFN_CONV - back, :]
        return out

    ug = conv(yg, tail[:SUBLANES], cg_ref[...])
    uu = conv(yu, tail[SUBLANES:], cu_ref[...])
    tail_ref[j] = jnp.concatenate([yg[tm - SUBLANES:], yu[tm - SUBLANES:]], axis=0)
    o_ref[...] = (ug * _sigmoid(ug) * uu).astype(o_ref.dtype)


def _ffn_up(h, w_up, conv_w, seq, tm=1024, tn=256):
    m, d = h.shape
    nj = FFN_DIM // tn
    est = 2 * (tm * d * 2 + 2 * d * tn * 2 + tm * tn * 2) + nj * 2 * SUBLANES * tn * 4 + 12 * tm * tn * 4
    return pl.pallas_call(
        functools.partial(_ffn_up_kernel, blocks_per_seq=seq // tm),
        grid=(m // tm, nj),
        in_specs=[pl.BlockSpec((tm, d), lambda i, j: (i, 0)),
                  pl.BlockSpec((d, tn), lambda i, j: (0, j)),
                  pl.BlockSpec((d, tn), lambda i, j: (0, nj + j)),
                  pl.BlockSpec((FFN_CONV, tn), lambda i, j: (0, j)),
                  pl.BlockSpec((FFN_CONV, tn), lambda i, j: (0, nj + j))],
        out_specs=pl.BlockSpec((tm, tn), lambda i, j: (i, j)),
        out_shape=jax.ShapeDtypeStruct((m, FFN_DIM), BF16),
        scratch_shapes=[pltpu.VMEM((nj, 2 * SUBLANES, tn), F32)],
        compiler_params=_params(("arbitrary", "arbitrary"), est),
        name="ffn_up_conv_gate",
    )(h, w_up, w_up, conv_w, conv_w)


def _layer(x, p, batch, seq):
    h = _rmsnorm(x, p["norm_mix_gain"], BF16)
    big = _matmul([h], [p["w_big"]], BF16, tm=1024, tn=512, name="in_proj")
    small = _matmul([h], [p["w_small"]], F32, tm=1024, tn=LANES, name="gate_proj")
    gates, gates_t = _gates(small, p["gate_bias"], p["gate_alog"], batch, seq)
    y_pool = _pool(big, p["pool_w"], p["pool_scale"], batch, seq)
    y_fox = _fox_attention(big, gates, gates_t, batch, seq)
    y_gdn = _gdn(big, gates, gates_t, p["gdn_conv_w"], p["gdn_norm_gain"], batch, seq)
    x = _matmul([y_pool, y_fox, y_gdn], p["w_o_parts"], F32, tm=1024, tn=512, residual=x, name="out_proj")
    h = _rmsnorm(x, p["norm_ffn_gain"], BF16)
    act = _ffn_up(h, p["w_up"], p["ffn_conv_w"], seq)
    return _matmul([act], [p["w_down"]], F32, tm=1024, tn=512, residual=x, k_steps=2, name="ffn_down")


def _layer_params(l, norm_mix_gain, w_in, pool_w, pool_scale, fox_f_bias, gdn_conv_w, gdn_A_log, gdn_dt_bias,
                  gdn_norm_gain, w_o, norm_ffn_gain, w_up, ffn_conv_w, w_down):
    sizes = (POOL_DIM, 3 * ATTN_DIM, FOX_HEADS, 3 * GDN_DIM, GDN_DIM, GDN_HEADS, GDN_HEADS)
    offs = [0]
    for s in sizes:
        offs.append(offs[-1] + s)
    w = w_in[l]
    cols = lambda i: w[:, offs[i]:offs[i + 1]]
    w_big = jnp.concatenate([cols(0), cols(1), cols(3), cols(4)], axis=1).astype(BF16)
    pad = LANES - (FOX_HEADS + 2 * GDN_HEADS)
    w_small = jnp.concatenate([cols(2), cols(5), cols(6), jnp.zeros((D_MODEL, pad), w.dtype)], axis=1).astype(BF16)
    zeros_h = jnp.zeros((GDN_HEADS,), F32)
    zeros_pad = jnp.zeros((pad,), F32)
    gate_bias = jnp.concatenate([fox_f_bias[l].astype(F32), zeros_h, gdn_dt_bias[l].astype(F32), zeros_pad])
    gate_alog = jnp.concatenate([jnp.zeros((FOX_HEADS,), F32), zeros_h, gdn_A_log[l].astype(F32), zeros_pad])
    wo = w_o[l].astype(BF16)
    return {
        "norm_mix_gain": norm_mix_gain[l], "w_big": w_big, "w_small": w_small,
        "gate_bias": gate_bias, "gate_alog": gate_alog,
        "pool_w": pool_w[l].astype(BF16), "pool_scale": pool_scale[l],
        "gdn_conv_w": gdn_conv_w[l], "gdn_norm_gain": gdn_norm_gain[l],
        "w_o_parts": [wo[:POOL_DIM], wo[POOL_DIM:POOL_DIM + ATTN_DIM], wo[POOL_DIM + ATTN_DIM:]],
        "norm_ffn_gain": norm_ffn_gain[l], "w_up": w_up[l].astype(BF16),
        "ffn_conv_w": ffn_conv_w[l], "w_down": w_down[l].astype(BF16),
    }


def kernel(x, norm_mix_gain, w_in, pool_w, pool_scale, fox_f_bias, gdn_conv_w, gdn_A_log, gdn_dt_bias,
           gdn_norm_gain, w_o, norm_ffn_gain, w_up, ffn_conv_w, w_down, final_norm_gain):
    batch, seq, d = x.shape
    assert d == D_MODEL and seq % 1024 == 0
    xf = x.reshape(batch * seq, d).astype(F32)
    for l in range(norm_mix_gain.shape[0]):
        p = _layer_params(l, norm_mix_gain, w_in, pool_w, pool_scale, fox_f_bias, gdn_conv_w, gdn_A_log,
                          gdn_dt_bias, gdn_norm_gain, w_o, norm_ffn_gain, w_up, ffn_conv_w, w_down)
        xf = _layer(xf, p, batch, seq)
    out = _rmsnorm(xf, final_norm_gain, x.dtype)
    return out.reshape(batch, seq, d)
```

```python
import functools
import math

import jax
import jax.numpy as jnp
from jax import lax
from jax.experimental import pallas as pl
from jax.experimental.pallas import tpu as pltpu

D_MODEL = 4096
HEAD_DIM = 128
POOL_WINDOWS = (2, 4, 8, 16)
POOL_GROUPS = 4
POOL_GROUP_DIM = D_MODEL // 16
POOL_DIM = POOL_GROUPS * POOL_GROUP_DIM
ATTN_DIM = (D_MODEL - POOL_DIM) // 2
FOX_HEADS = ATTN_DIM // HEAD_DIM
GDN_DIM = D_MODEL - POOL_DIM - ATTN_DIM
GDN_HEADS = GDN_DIM // HEAD_DIM
GDN_CONV = 4
FFN_DIM = 11008
FFN_CONV = 3
EPS = 1e-6
IN_DIM = POOL_DIM + 3 * ATTN_DIM + FOX_HEADS + 3 * GDN_DIM + GDN_DIM + 2 * GDN_HEADS

LANES = 128
SUBLANES = 8
VMEM_BYTES_V7X = 64 * 1024 * 1024
VMEM_CAP = VMEM_BYTES_V7X - 8 * 1024 * 1024

ALIGNED_DIM = POOL_DIM + 3 * ATTN_DIM
BIG_DIM = ALIGNED_DIM + 3 * GDN_DIM + GDN_DIM
N_GATES = FOX_HEADS + 2 * GDN_HEADS
TAIL_DIM = IN_DIM - BIG_DIM
FOX_COL0 = POOL_DIM // LANES
GDN_COL0 = ALIGNED_DIM // LANES
Z_COL0 = (ALIGNED_DIM + 3 * GDN_DIM) // LANES
MIX_FOX_COL0 = POOL_DIM // LANES
MIX_GDN_COL0 = (POOL_DIM + ATTN_DIM) // LANES
LANE_F = 0
LANE_BETA = FOX_HEADS
LANE_G = FOX_HEADS + GDN_HEADS
GATE_ROWS = 48
GDN_CHUNK = 128

F32 = jnp.float32
BF16 = jnp.bfloat16
NT_DIMS = (((1,), (1,)), ((), ()))
TN_DIMS = (((0,), (0,)), ((), ()))


def _params(semantics, vmem_estimate):
    limit = min(int(vmem_estimate * 1.25) + (4 << 20), VMEM_CAP)
    return pltpu.CompilerParams(dimension_semantics=semantics, vmem_limit_bytes=limit)


def _sigmoid(x):
    return 1.0 / (1.0 + jnp.exp(-x))


def _softplus(x):
    return jnp.maximum(x, 0.0) + jnp.log1p(jnp.exp(-jnp.abs(x)))


def _regroup_kernel(main_ref, tail_ref, big_ref, small_ref):
    tk = main_ref.shape[0]
    big_ref[:, :ALIGNED_DIM] = main_ref[:, :ALIGNED_DIM].astype(BF16)
    r = lax.broadcasted_iota(jnp.int32, (2 * LANES, LANES), 0)
    c = lax.broadcasted_iota(jnp.int32, (2 * LANES, LANES), 1)
    shift_mat = jnp.where(r == c + FOX_HEADS, 1.0, 0.0).astype(BF16)
    for t in range((BIG_DIM - ALIGNED_DIM) // LANES):
        src = ALIGNED_DIM + t * LANES
        if src + 2 * LANES <= BIG_DIM:
            win = main_ref[:, src:src + 2 * LANES]
        else:
            win = jnp.concatenate([main_ref[:, src:src + LANES], tail_ref[...]], axis=1)
        moved = jnp.dot(win.astype(BF16), shift_mat, preferred_element_type=F32)
        big_ref[:, src:src + LANES] = moved.astype(BF16)
    lane = lax.broadcasted_iota(jnp.int32, (tk, LANES), 1)
    gate_cols = jnp.where(lane < FOX_HEADS, main_ref[:, ALIGNED_DIM:ALIGNED_DIM + LANES], tail_ref[...])
    small_ref[...] = jnp.where(lane < N_GATES, gate_cols, 0.0).astype(BF16)


def _regroup_w_in(w_in, tk=256):
    n_layers, d, _ = w_in.shape
    tail = jnp.pad(w_in[:, :, BIG_DIM:], ((0, 0), (0, 0), (0, LANES - TAIL_DIM)))
    est = 2 * tk * BIG_DIM * (4 + 2) + 4 * tk * LANES * 4
    return pl.pallas_call(
        _regroup_kernel,
        grid=(n_layers, d // tk),
        in_specs=[pl.BlockSpec((None, tk, BIG_DIM), lambda l, i: (l, i, 0)),
                  pl.BlockSpec((None, tk, LANES), lambda l, i: (l, i, 0))],
        out_specs=[pl.BlockSpec((None, tk, BIG_DIM), lambda l, i: (l, i, 0)),
                   pl.BlockSpec((None, tk, LANES), lambda l, i: (l, i, 0))],
        out_shape=[jax.ShapeDtypeStruct((n_layers, d, BIG_DIM), BF16),
                   jax.ShapeDtypeStruct((n_layers, d, LANES), BF16)],
        compiler_params=_params(("arbitrary", "arbitrary"), est),
        name="regroup_w_in",
    )(w_in, tail)


def _rms_kernel(x_ref, g_ref, o_ref):
    x = x_ref[...]
    ms = jnp.mean(x * x, axis=-1, keepdims=True)
    o_ref[...] = (x * lax.rsqrt(ms + EPS) * g_ref[...]).astype(o_ref.dtype)


def _rmsnorm(x, gain, out_dtype, tm=512):
    m, d = x.shape
    est = 2 * tm * d * (4 + jnp.dtype(out_dtype).itemsize)
    return pl.pallas_call(
        _rms_kernel,
        grid=(m // tm,),
        in_specs=[pl.BlockSpec((tm, d), lambda i: (i, 0)),
                  pl.BlockSpec((1, d), lambda i: (0, 0))],
        out_specs=pl.BlockSpec((tm, d), lambda i: (i, 0)),
        out_shape=jax.ShapeDtypeStruct((m, d), out_dtype),
        compiler_params=_params(("arbitrary",), est),
        name="rmsnorm",
    )(x, gain.reshape(1, d))


def _mm_kernel(*refs, has_res, k_steps):
    a_ref, b_ref = refs[0], refs[1]
    res_ref = refs[2] if has_res else None
    o_ref = refs[-1]
    acc = jnp.dot(a_ref[...], b_ref[...], preferred_element_type=F32)
    if k_steps == 1:
        if has_res:
            acc = acc + res_ref[...]
        o_ref[...] = acc.astype(o_ref.dtype)
    else:
        k = pl.program_id(2)

        @pl.when(k == 0)
        def _():
            o_ref[...] = (acc + res_ref[...]) if has_res else acc

        @pl.when(k > 0)
        def _():
            o_ref[...] += acc


def _matmul(a, w, layer, out_dtype, tm, tn, residual=None, k_steps=1, name="matmul"):
    m, kdim = a.shape
    n = w.shape[2]
    tk = kdim // k_steps
    has_res = residual is not None
    assert k_steps == 1 or out_dtype == F32
    in_specs = [pl.BlockSpec((tm, tk), lambda i, j, k: (i, k)),
                pl.BlockSpec((None, tk, tn), lambda i, j, k: (layer, k, j))]
    args = [a, w]
    est = 2 * tm * tk * a.dtype.itemsize + 2 * tk * tn * w.dtype.itemsize
    if has_res:
        in_specs.append(pl.BlockSpec((tm, tn), lambda i, j, k: (i, j)))
        est += 2 * tm * tn * 4
        args.append(residual)
    est += 2 * tm * tn * jnp.dtype(out_dtype).itemsize + 2 * tm * tn * 4
    return pl.pallas_call(
        functools.partial(_mm_kernel, has_res=has_res, k_steps=k_steps),
        grid=(m // tm, n // tn, k_steps),
        in_specs=in_specs,
        out_specs=pl.BlockSpec((tm, tn), lambda i, j, k: (i, j)),
        out_shape=jax.ShapeDtypeStruct((m, n), out_dtype),
        compiler_params=_params(("arbitrary", "arbitrary", "arbitrary"), est),
        name=name,
    )(*args)


def _scan_rows(y, row, seg):
    pos = row & (seg - 1)
    s = 1
    while s < seg:
        y = y + jnp.where(pos >= s, pltpu.roll(y, s, axis=0), 0.0)
        s *= 2
    return y


def _gates_kernel(x_ref, p_ref, g_ref, gt_ref, carry_ref):
    t = pl.program_id(1)

    @pl.when(t == 0)
    def _():
        carry_ref[...] = jnp.zeros_like(carry_ref)

    tb = x_ref.shape[0]
    z = x_ref[...] + p_ref[0:1, :]
    lane = lax.broadcasted_iota(jnp.int32, z.shape, 1)
    row = lax.broadcasted_iota(jnp.int32, z.shape, 0)
    log_f = -_softplus(-z)
    beta = _sigmoid(z)
    g = -jnp.exp(p_ref[1:2, :]) * _softplus(z)
    cum_f = _scan_rows(log_f, row, tb) + carry_ref[0:1, :]
    carry_ref[0:1, :] = cum_f[tb - 1:tb, :]
    cum_g = _scan_rows(g, row, GDN_CHUNK)
    out = jnp.where(lane < LANE_BETA, cum_f, jnp.where(lane < LANE_G, beta, cum_g))
    g_ref[...] = out
    gt_ref[...] = out.T[:GATE_ROWS, :]


def _gates(small, bias_row, alog_row, batch, seq, tb=512):
    m = small.shape[0]
    nt = seq // tb
    params = jnp.zeros((SUBLANES, LANES), F32).at[0].set(bias_row).at[1].set(alog_row)
    est = 2 * tb * LANES * 4 * 3 + 16 * tb * LANES * 4
    return pl.pallas_call(
        _gates_kernel,
        grid=(batch, nt),
        in_specs=[pl.BlockSpec((tb, LANES), lambda b, t: (b * nt + t, 0)),
                  pl.BlockSpec((SUBLANES, LANES), lambda b, t: (0, 0))],
        out_specs=[pl.BlockSpec((tb, LANES), lambda b, t: (b * nt + t, 0)),
                   pl.BlockSpec((GATE_ROWS, tb), lambda b, t: (0, b * nt + t))],
        out_shape=[jax.ShapeDtypeStruct((m, LANES), F32),
                   jax.ShapeDtypeStruct((GATE_ROWS, m), F32)],
        scratch_shapes=[pltpu.VMEM((SUBLANES, LANES), F32)],
        compiler_params=_params(("arbitrary", "arbitrary"), est),
        name="gates",
    )(small, params)


def _pool_kernel(x_ref, w_ref, sc_ref, o_ref, tail_ref):
    t = pl.program_id(1)

    @pl.when(t == 0)
    def _():
        tail_ref[...] = jnp.zeros_like(tail_ref)

    tb = x_ref.shape[0]
    halo = tail_ref.shape[0]
    cg = POOL_GROUP_DIM
    x = x_ref[...].astype(F32)
    xe = jnp.concatenate([tail_ref[...], x], axis=0)
    tail_ref[...] = x[tb - halo:, :]
    pos = (t * tb + 1 + lax.broadcasted_iota(jnp.int32, (tb, cg), 0)).astype(F32)
    for gi, win in enumerate(POOL_WINDOWS):
        s = xe[:, gi * cg:(gi + 1) * cg]
        span = 1
        while span < win:
            s = s + pltpu.roll(s, span, axis=0)
            span *= 2
        mean = s[halo:, :] / jnp.minimum(pos, float(win))
        pooled = (mean - x[:, gi * cg:(gi + 1) * cg]).astype(BF16)
        y = jnp.dot(pooled, w_ref[gi], preferred_element_type=F32)
        o_ref[:, gi * cg:(gi + 1) * cg] = (y * sc_ref[:, gi * cg:(gi + 1) * cg]).astype(o_ref.dtype)


def _pool(big, pool_w, pool_scale, layer, batch, seq, tb=512):
    m = big.shape[0]
    nt = seq // tb
    halo = 16
    assert halo >= max(POOL_WINDOWS)
    est = 2 * tb * POOL_DIM * 4 + 8 * tb * POOL_DIM * 4
    return pl.pallas_call(
        _pool_kernel,
        grid=(batch, nt),
        in_specs=[pl.BlockSpec((tb, POOL_DIM), lambda b, t: (b * nt + t, 0)),
                  pl.BlockSpec((None, POOL_GROUPS, POOL_GROUP_DIM, POOL_GROUP_DIM), lambda b, t: (layer, 0, 0, 0)),
                  pl.BlockSpec((None, 1, POOL_DIM), lambda b, t: (layer, 0, 0))],
        out_specs=pl.BlockSpec((tb, POOL_DIM), lambda b, t: (b * nt + t, 0)),
        out_shape=jax.ShapeDtypeStruct((m, D_MODEL), BF16),
        scratch_shapes=[pltpu.VMEM((halo, POOL_DIM), F32)],
        compiler_params=_params(("arbitrary", "arbitrary"), est),
        name="pool_mixer",
    )(big, pool_w, pool_scale.reshape(pool_scale.shape[0], 1, POOL_DIM))


def _fox_kernel(q_ref, k_ref, v_ref, g_ref, cr_ref, mix_ref, o_ref, *, tq):
    del mix_ref
    h = pl.program_id(1)
    qi = pl.program_id(2)
    scale = 1.0 / math.sqrt(HEAD_DIM)
    q = q_ref[...]
    gates = g_ref[...]
    lane = lax.broadcasted_iota(jnp.int32, gates.shape, 1)
    cq = jnp.sum(jnp.where(lane == LANE_F + h, gates, 0.0), axis=1, keepdims=True)
    tri = (lax.broadcasted_iota(jnp.int32, (tq, tq), 0) >= lax.broadcasted_iota(jnp.int32, (tq, tq), 1))

    def step(ki, carry, masked):
        m_prev, l_prev, acc = carry
        start = pl.multiple_of(ki * tq, tq)
        kb = k_ref[pl.ds(start, tq), :]
        vb = v_ref[pl.ds(start, tq), :]
        s = lax.dot_general(q, kb, NT_DIMS, preferred_element_type=F32)
        zp = s * scale - cr_ref[ki]
        if masked:
            zp = jnp.where(tri, zp, -jnp.inf)
        m_new = jnp.maximum(m_prev, jnp.max(zp, axis=1, keepdims=True) + cq)
        p = jnp.exp(zp - (m_new - cq))
        alpha = jnp.exp(m_prev - m_new)
        l_new = alpha * l_prev + jnp.sum(p, axis=1, keepdims=True)
        acc = alpha * acc + jnp.dot(p.astype(BF16), vb, preferred_element_type=F32)
        return m_new, l_new, acc

    init = (jnp.full((tq, 1), -jnp.inf, F32), jnp.zeros((tq, 1), F32), jnp.zeros((tq, HEAD_DIM), F32))
    carry = lax.fori_loop(0, qi, lambda ki, c: step(ki, c, False), init)
    _, l_fin, acc = step(qi, carry, True)
    o_ref[...] = (acc / l_fin).astype(o_ref.dtype)


def _fox_attention(big, gates, gates_t, mix, batch, seq, tq=512):
    m = big.shape[0]
    nq = seq // tq
    cr = gates_t.reshape(GATE_ROWS, m // tq, 1, tq)
    est = 2 * (tq * HEAD_DIM * 2 * 2 + 2 * seq * HEAD_DIM * 2 + tq * LANES * 4 + seq * 4 * 8) + 10 * tq * tq * 4
    return pl.pallas_call(
        functools.partial(_fox_kernel, tq=tq),
        grid=(batch, FOX_HEADS, nq),
        in_specs=[
            pl.BlockSpec((tq, HEAD_DIM), lambda b, h, i: (b * nq + i, FOX_COL0 + h)),
            pl.BlockSpec((seq, HEAD_DIM), lambda b, h, i: (b, FOX_COL0 + FOX_HEADS + h)),
            pl.BlockSpec((seq, HEAD_DIM), lambda b, h, i: (b, FOX_COL0 + 2 * FOX_HEADS + h)),
            pl.BlockSpec((tq, LANES), lambda b, h, i: (b * nq + i, 0)),
            pl.BlockSpec((None, nq, 1, tq), lambda b, h, i: (LANE_F + h, b, 0, 0)),
            pl.BlockSpec(memory_space=pl.ANY),
        ],
        out_specs=pl.BlockSpec((tq, HEAD_DIM), lambda b, h, i: (b * nq + i, MIX_FOX_COL0 + h)),
        out_shape=jax.ShapeDtypeStruct(mix.shape, mix.dtype),
        input_output_aliases={5: 0},
        compiler_params=_params(("arbitrary", "arbitrary", "arbitrary"), est),
        name="fox_attention",
    )(big, big, big, gates, cr, mix)


def _bdot(a, b):
    return jnp.dot(a.astype(BF16), b.astype(BF16), preferred_element_type=F32)


def _inv_unit_lower(lows, row, col):
    n = lows[0].shape[0]
    eye = (row == col).astype(F32)
    diag = (row >> 3) == (col >> 3)
    ds = [jnp.where(diag, low, 0.0) for low in lows]
    d2s = [_bdot(d, d) for d in ds]
    d4s = [_bdot(d2, d2) for d2 in d2s]
    invs = [eye - d for d in ds]
    invs = [inv + _bdot(inv, d2) for inv, d2 in zip(invs, d2s)]
    invs = [inv + _bdot(inv, d4) for inv, d4 in zip(invs, d4s)]
    shift = 3
    while (1 << shift) < n:
        rb = row >> shift
        cb = col >> shift
        join = ((rb & 1) == 1) & (cb == rb - 1)
        inv16s = [inv.astype(BF16) for inv in invs]
        xs = [_bdot(jnp.where(join, low, 0.0), inv16) for low, inv16 in zip(lows, inv16s)]
        invs = [inv - _bdot(inv16, x) for inv, inv16, x in zip(invs, inv16s, xs)]
        shift += 1
    return invs


def _gdn_kernel(q_ref, k_ref, v_ref, z_ref, g_ref, gr_ref, cwq_ref, cwk_ref, cwv_ref, gain_ref, mix_ref,
                o_ref, state_ref, tail_ref):
    del mix_ref
    h = pl.program_id(1)
    t = pl.program_id(2)

    @pl.when(t == 0)
    def _():
        state_ref[...] = jnp.zeros_like(state_ref)
        tail_ref[...] = jnp.zeros_like(tail_ref)

    tb = q_ref.shape[0]
    c = GDN_CHUNK
    dk = HEAD_DIM

    def conv_silu(x_ref, w_ref, slot):
        x = x_ref[...].astype(F32)
        xe = jnp.concatenate([tail_ref[slot], x], axis=0)
        w = w_ref[...]
        y = x * w[GDN_CONV - 1:GDN_CONV, :]
        for back in range(1, GDN_CONV):
            y = y + pltpu.roll(xe, back, axis=0)[SUBLANES:, :] * w[GDN_CONV - 1 - back:GDN_CONV - back, :]
        tail_ref[slot] = x[tb - SUBLANES:, :]
        return y * _sigmoid(y)

    def l2n(x):
        return x * lax.rsqrt(jnp.sum(x * x, axis=-1, keepdims=True) + EPS)

    q_all = l2n(conv_silu(q_ref, cwq_ref, 0)) * (dk ** -0.5)
    k_all = l2n(conv_silu(k_ref, cwk_ref, 1))
    v_all = conv_silu(v_ref, cwv_ref, 2)

    gates = g_ref[...]
    lane = lax.broadcasted_iota(jnp.int32, gates.shape, 1)
    beta_all = jnp.sum(jnp.where(lane == LANE_BETA + h, gates, 0.0), axis=1, keepdims=True)
    gcol_all = jnp.sum(jnp.where(lane == LANE_G + h, gates, 0.0), axis=1, keepdims=True)
    grow_all = gr_ref[...]
    kbeta_all = k_all * beta_all
    k16_all = k_all.astype(BF16)

    row = lax.broadcasted_iota(jnp.int32, (c, c), 0)
    col = lax.broadcasted_iota(jnp.int32, (c, c), 1)
    causal = row >= col
    strict = row > col
    last_lane = lax.broadcasted_iota(jnp.int32, (1, c), 1) == c - 1

    chunks = [slice(ci * c, (ci + 1) * c) for ci in range(tb // c)]
    decays = [jnp.exp(jnp.where(causal, gcol_all[sl] - grow_all[:, sl], -jnp.inf)) for sl in chunks]
    a_mats = [jnp.where(strict, lax.dot_general(kbeta_all[sl].astype(BF16), k16_all[sl], NT_DIMS,
                                                preferred_element_type=F32) * dec, 0.0)
              for sl, dec in zip(chunks, decays)]
    intras = [jnp.where(causal, lax.dot_general(q_all[sl].astype(BF16), k16_all[sl], NT_DIMS,
                                                preferred_element_type=F32) * dec, 0.0).astype(BF16)
              for sl, dec in zip(chunks, decays)]
    invs = _inv_unit_lower(a_mats, row, col)
    exp_g = jnp.exp(gcol_all)
    rhs_all = jnp.concatenate([v_all * beta_all, kbeta_all * exp_g], axis=1)
    sols = [_bdot(inv, rhs_all[sl]) for inv, sl in zip(invs, chunks)]
    q_dec_all = q_all * exp_g

    state = state_ref[...]
    for sl, sol, intra in zip(chunks, sols, intras):
        u, w = sol[:, :dk], sol[:, dk:]
        g_last = jnp.sum(jnp.where(last_lane, grow_all[:, sl], 0.0), axis=1, keepdims=True)
        k_dec = k_all[sl] * jnp.exp(g_last - gcol_all[sl])
        wq = jnp.concatenate([w, q_dec_all[sl]], axis=0).astype(BF16)
        ws = jnp.dot(wq, state.astype(BF16), preferred_element_type=F32)
        v_new = u - ws[:c]
        vn16 = v_new.astype(BF16)
        o = ws[c:] + jnp.dot(intra, vn16, preferred_element_type=F32)
        state = state * jnp.exp(g_last) + lax.dot_general(k_dec.astype(BF16), vn16, TN_DIMS,
                                                          preferred_element_type=F32)
        z = z_ref[sl, :].astype(F32)
        o = o * lax.rsqrt(jnp.mean(o * o, axis=-1, keepdims=True) + EPS) * gain_ref[...]
        o_ref[sl, :] = (o * (z * _sigmoid(z))).astype(o_ref.dtype)
    state_ref[...] = state


def _gdn(big, gates, gates_t, conv_w, norm_gain, mix, layer, batch, seq, tb=512):
    m = big.shape[0]
    nt = seq // tb
    gr = gates_t.reshape(GATE_ROWS, 1, m)
    est = 2 * (5 * tb * HEAD_DIM * 2 + tb * LANES * 4 + tb * 32) + 64 * tb * HEAD_DIM * 4
    blk = lambda off: pl.BlockSpec((tb, HEAD_DIM), lambda b, h, t: (b * nt + t, off + h))
    cw = lambda off: pl.BlockSpec((None, GDN_CONV, HEAD_DIM), lambda b, h, t: (layer, 0, off + h))
    return pl.pallas_call(
        _gdn_kernel,
        grid=(batch, GDN_HEADS, nt),
        in_specs=[
            blk(GDN_COL0), blk(GDN_COL0 + GDN_HEADS), blk(GDN_COL0 + 2 * GDN_HEADS), blk(Z_COL0),
            pl.BlockSpec((tb, LANES), lambda b, h, t: (b * nt + t, 0)),
            pl.BlockSpec((None, 1, tb), lambda b, h, t: (LANE_G + h, 0, b * nt + t)),
            cw(0), cw(GDN_HEADS), cw(2 * GDN_HEADS),
            pl.BlockSpec((None, 1, HEAD_DIM), lambda b, h, t: (layer, 0, 0)),
            pl.BlockSpec(memory_space=pl.ANY),
        ],
        out_specs=pl.BlockSpec((tb, HEAD_DIM), lambda b, h, t: (b * nt + t, MIX_GDN_COL0 + h)),
        out_shape=jax.ShapeDtypeStruct(mix.shape, mix.dtype),
        input_output_aliases={10: 0},
        scratch_shapes=[pltpu.VMEM((HEAD_DIM, HEAD_DIM), F32),
                        pltpu.VMEM((3, SUBLANES, HEAD_DIM), F32)],
        compiler_params=_params(("arbitrary", "arbitrary", "arbitrary"), est),
        name="gated_delta_rule",
    )(big, big, big, big, gates, gr, conv_w, conv_w, conv_w,
      norm_gain.reshape(norm_gain.shape[0], 1, HEAD_DIM), mix)


def _ffn_up_kernel(h_ref, wg_ref, wu_ref, cg_ref, cu_ref, o_ref, tail_ref, *, blocks_per_seq):
    i = pl.program_id(0)
    j = pl.program_id(1)
    tm = h_ref.shape[0]

    @pl.when(i % blocks_per_seq == 0)
    def _():
        tail_ref[j] = jnp.zeros(tail_ref.shape[1:], F32)

    h = h_ref[...]
    yg = jnp.dot(h, wg_ref[...], preferred_element_type=F32)
    yu = jnp.dot(h, wu_ref[...], preferred_element_type=F32)
    tail = tail_ref[j]

    def conv(y, prev, w):
        ye = jnp.concatenate([prev, y], axis=0)
        out = y * w[FFN_CONV - 1:FFN_CONV, :]
        for back in range(1, FFN_CONV):
            out = out + pltpu.roll(ye, back, axis=0)[SUBLANES:, :] * w[FFN_CONV - 1 - back:FFN_CONV - back, :]
        return out

    ug = conv(yg, tail[:SUBLANES], cg_ref[...])
    uu = conv(yu, tail[SUBLANES:], cu_ref[...])
    tail_ref[j] = jnp.concatenate([yg[tm - SUBLANES:], yu[tm - SUBLANES:]], axis=0)
    o_ref[...] = (ug * _sigmoid(ug) * uu).astype(o_ref.dtype)


def _ffn_up(h, w_up, conv_w, layer, seq, tm=1024, tn=256):
    m, d = h.shape
    nj = FFN_DIM // tn
    est = 2 * (tm * d * 2 + 2 * d * tn * 2 + tm * tn * 2) + nj * 2 * SUBLANES * tn * 4 + 12 * tm * tn * 4
    return pl.pallas_call(
        functools.partial(_ffn_up_kernel, blocks_per_seq=seq // tm),
        grid=(m // tm, nj),
        in_specs=[pl.BlockSpec((tm, d), lambda i, j: (i, 0)),
                  pl.BlockSpec((None, d, tn), lambda i, j: (layer, 0, j)),
                  pl.BlockSpec((None, d, tn), lambda i, j: (layer, 0, nj + j)),
                  pl.BlockSpec((None, FFN_CONV, tn), lambda i, j: (layer, 0, j)),
                  pl.BlockSpec((None, FFN_CONV, tn), lambda i, j: (layer, 0, nj + j))],
        out_specs=pl.BlockSpec((tm, tn), lambda i, j: (i, j)),
        out_shape=jax.ShapeDtypeStruct((m, FFN_DIM), BF16),
        scratch_shapes=[pltpu.VMEM((nj, 2 * SUBLANES, tn), F32)],
        compiler_params=_params(("arbitrary", "arbitrary"), est),
        name="ffn_up_conv_gate",
    )(h, w_up, w_up, conv_w, conv_w)


def kernel(x, norm_mix_gain, w_in, pool_w, pool_scale, fox_f_bias, gdn_conv_w, gdn_A_log, gdn_dt_bias,
           gdn_norm_gain, w_o, norm_ffn_gain, w_up, ffn_conv_w, w_down, final_norm_gain):
    batch, seq, d = x.shape
    n_layers = norm_mix_gain.shape[0]
    assert d == D_MODEL and w_in.shape[2] == IN_DIM and seq % 1024 == 0
    w_big, w_small = _regroup_w_in(w_in)
    w_o16, w_up16, w_down16, pool_w16 = (w.astype(BF16) for w in (w_o, w_up, w_down, pool_w))
    zeros_h = jnp.zeros((n_layers, GDN_HEADS), F32)
    zeros_pad = jnp.zeros((n_layers, LANES - N_GATES), F32)
    gate_bias = jnp.concatenate([fox_f_bias.astype(F32), zeros_h, gdn_dt_bias.astype(F32), zeros_pad], axis=1)
    gate_alog = jnp.concatenate([jnp.zeros((n_layers, FOX_HEADS), F32), zeros_h, gdn_A_log.astype(F32), zeros_pad],
                                axis=1)

    xf = x.reshape(batch * seq, d).astype(F32)
    for l in range(n_layers):
        h = _rmsnorm(xf, norm_mix_gain[l], BF16)
        big = _matmul(h, w_big, l, BF16, tm=1024, tn=512, name="in_proj")
        small = _matmul(h, w_small, l, F32, tm=1024, tn=LANES, name="gate_proj")
        gates, gates_t = _gates(small, gate_bias[l], gate_alog[l], batch, seq)
        mix = _pool(big, pool_w16, pool_scale, l, batch, seq)
        mix = _fox_attention(big, gates, gates_t, mix, batch, seq)
        mix = _gdn(big, gates, gates_t, gdn_conv_w, gdn_norm_gain, mix, l, batch, seq)
        xf = _matmul(mix, w_o16, l, F32, tm=1024, tn=512, residual=xf, name="out_proj")
        h = _rmsnorm(xf, norm_ffn_gain[l], BF16)
        act = _ffn_up(h, w_up16, ffn_conv_w, l, seq)
        xf = _matmul(act, w_down16, l, F32, tm=1024, tn=512, residual=xf, k_steps=2, name="ffn_down")
    out = _rmsnorm(xf, final_norm_gain, x.dtype)
    return out.reshape(batch, seq, d)
```

```python
import functools
import math

import jax
import jax.numpy as jnp
from jax import lax
from jax.experimental import pallas as pl
from jax.experimental.pallas import tpu as pltpu

D_MODEL = 4096
HEAD_DIM = 128
POOL_WINDOWS = (2, 4, 8, 16)
POOL_GROUPS = 4
POOL_GROUP_DIM = D_MODEL // 16
POOL_DIM = POOL_GROUPS * POOL_GROUP_DIM
ATTN_DIM = (D_MODEL - POOL_DIM) // 2
FOX_HEADS = ATTN_DIM // HEAD_DIM
GDN_DIM = D_MODEL - POOL_DIM - ATTN_DIM
GDN_HEADS = GDN_DIM // HEAD_DIM
GDN_CONV = 4
FFN_DIM = 11008
FFN_CONV = 3
EPS = 1e-6
IN_DIM = POOL_DIM + 3 * ATTN_DIM + FOX_HEADS + 3 * GDN_DIM + GDN_DIM + 2 * GDN_HEADS

LANES = 128
SUBLANES = 8
VMEM_BYTES_V7X = 64 * 1024 * 1024
VMEM_CAP = VMEM_BYTES_V7X - 8 * 1024 * 1024

ALIGNED_DIM = POOL_DIM + 3 * ATTN_DIM
BIG_DIM = ALIGNED_DIM + 3 * GDN_DIM + GDN_DIM
N_GATES = FOX_HEADS + 2 * GDN_HEADS
TAIL_DIM = IN_DIM - BIG_DIM
FOX_COL0 = POOL_DIM // LANES
GDN_COL0 = ALIGNED_DIM // LANES
Z_COL0 = (ALIGNED_DIM + 3 * GDN_DIM) // LANES
MIX_FOX_COL0 = POOL_DIM // LANES
MIX_GDN_COL0 = (POOL_DIM + ATTN_DIM) // LANES
LANE_F = 0
LANE_BETA = FOX_HEADS
LANE_G = FOX_HEADS + GDN_HEADS
GATE_ROWS = 48
GDN_CHUNK = 128

F32 = jnp.float32
BF16 = jnp.bfloat16
NT_DIMS = (((1,), (1,)), ((), ()))
TN_DIMS = (((0,), (0,)), ((), ()))


def _params(semantics, vmem_estimate):
    limit = min(int(vmem_estimate * 1.25) + (4 << 20), VMEM_CAP)
    return pltpu.CompilerParams(dimension_semantics=semantics, vmem_limit_bytes=limit)


def _sigmoid(x):
    return 1.0 / (1.0 + jnp.exp(-x))


def _softplus(x):
    return jnp.maximum(x, 0.0) + jnp.log1p(jnp.exp(-jnp.abs(x)))


def _regroup_kernel(main_ref, tail_ref, big_ref, small_ref):
    tk = main_ref.shape[0]
    big_ref[:, :ALIGNED_DIM] = main_ref[:, :ALIGNED_DIM].astype(BF16)
    r = lax.broadcasted_iota(jnp.int32, (2 * LANES, LANES), 0)
    c = lax.broadcasted_iota(jnp.int32, (2 * LANES, LANES), 1)
    shift_mat = jnp.where(r == c + FOX_HEADS, 1.0, 0.0).astype(BF16)
    for t in range((BIG_DIM - ALIGNED_DIM) // LANES):
        src = ALIGNED_DIM + t * LANES
        if src + 2 * LANES <= BIG_DIM:
            win = main_ref[:, src:src + 2 * LANES]
        else:
            win = jnp.concatenate([main_ref[:, src:src + LANES], tail_ref[...]], axis=1)
        moved = jnp.dot(win.astype(BF16), shift_mat, preferred_element_type=F32)
        big_ref[:, src:src + LANES] = moved.astype(BF16)
    lane = lax.broadcasted_iota(jnp.int32, (tk, LANES), 1)
    gate_cols = jnp.where(lane < FOX_HEADS, main_ref[:, ALIGNED_DIM:ALIGNED_DIM + LANES], tail_ref[...])
    small_ref[...] = jnp.where(lane < N_GATES, gate_cols, 0.0).astype(BF16)


def _regroup_w_in(w_in, tk=256):
    n_layers, d, _ = w_in.shape
    tail = jnp.pad(w_in[:, :, BIG_DIM:], ((0, 0), (0, 0), (0, LANES - TAIL_DIM)))
    est = 2 * tk * BIG_DIM * (4 + 2) + 4 * tk * LANES * 4
    return pl.pallas_call(
        _regroup_kernel,
        grid=(n_layers, d // tk),
        in_specs=[pl.BlockSpec((None, tk, BIG_DIM), lambda l, i: (l, i, 0)),
                  pl.BlockSpec((None, tk, LANES), lambda l, i: (l, i, 0))],
        out_specs=[pl.BlockSpec((None, tk, BIG_DIM), lambda l, i: (l, i, 0)),
                   pl.BlockSpec((None, tk, LANES), lambda l, i: (l, i, 0))],
        out_shape=[jax.ShapeDtypeStruct((n_layers, d, BIG_DIM), BF16),
                   jax.ShapeDtypeStruct((n_layers, d, LANES), BF16)],
        compiler_params=_params(("arbitrary", "arbitrary"), est),
        name="regroup_w_in",
    )(w_in, tail)


def _rms_kernel(x_ref, g_ref, o_ref):
    x = x_ref[...]
    ms = jnp.mean(x * x, axis=-1, keepdims=True)
    o_ref[...] = (x * lax.rsqrt(ms + EPS) * g_ref[...]).astype(o_ref.dtype)


def _rmsnorm(x, gain, out_dtype, tm=512):
    m, d = x.shape
    est = 2 * tm * d * (4 + jnp.dtype(out_dtype).itemsize)
    return pl.pallas_call(
        _rms_kernel,
        grid=(m // tm,),
        in_specs=[pl.BlockSpec((tm, d), lambda i: (i, 0)),
                  pl.BlockSpec((1, d), lambda i: (0, 0))],
        out_specs=pl.BlockSpec((tm, d), lambda i: (i, 0)),
        out_shape=jax.ShapeDtypeStruct((m, d), out_dtype),
        compiler_params=_params(("arbitrary",), est),
        name="rmsnorm",
    )(x, gain.reshape(1, d))


def _mm_kernel(*refs, has_res, k_steps):
    a_ref, b_ref = refs[0], refs[1]
    res_ref = refs[2] if has_res else None
    o_ref = refs[-1]
    acc = jnp.dot(a_ref[...], b_ref[...], preferred_element_type=F32)
    if k_steps == 1:
        if has_res:
            acc = acc + res_ref[...]
        o_ref[...] = acc.astype(o_ref.dtype)
    else:
        k = pl.program_id(2)

        @pl.when(k == 0)
        def _():
            o_ref[...] = (acc + res_ref[...]) if has_res else acc

        @pl.when(k > 0)
        def _():
            o_ref[...] += acc


def _matmul(a, w, layer, out_dtype, tm, tn, residual=None, k_steps=1, name="matmul"):
    m, kdim = a.shape
    n = w.shape[2]
    tk = kdim // k_steps
    has_res = residual is not None
    assert k_steps == 1 or out_dtype == F32
    in_specs = [pl.BlockSpec((tm, tk), lambda i, j, k: (i, k)),
                pl.BlockSpec((None, tk, tn), lambda i, j, k: (layer, k, j))]
    args = [a, w]
    est = 2 * tm * tk * a.dtype.itemsize + 2 * tk * tn * w.dtype.itemsize
    if has_res:
        in_specs.append(pl.BlockSpec((tm, tn), lambda i, j, k: (i, j)))
        est += 2 * tm * tn * 4
        args.append(residual)
    est += 2 * tm * tn * jnp.dtype(out_dtype).itemsize + 2 * tm * tn * 4
    return pl.pallas_call(
        functools.partial(_mm_kernel, has_res=has_res, k_steps=k_steps),
        grid=(m // tm, n // tn, k_steps),
        in_specs=in_specs,
        out_specs=pl.BlockSpec((tm, tn), lambda i, j, k: (i, j)),
        out_shape=jax.ShapeDtypeStruct((m, n), out_dtype),
        compiler_params=_params(("arbitrary", "arbitrary", "arbitrary"), est),
        name=name,
    )(*args)


def _scan_rows(y, row, seg):
    pos = row & (seg - 1)
    s = 1
    while s < seg:
        y = y + jnp.where(pos >= s, pltpu.roll(y, s, axis=0), 0.0)
        s *= 2
    return y


def _gates_kernel(x_ref, p_ref, g_ref, gt_ref, carry_ref):
    t = pl.program_id(1)

    @pl.when(t == 0)
    def _():
        carry_ref[...] = jnp.zeros_like(carry_ref)

    tb = x_ref.shape[0]
    z = x_ref[...] + p_ref[0:1, :]
    lane = lax.broadcasted_iota(jnp.int32, z.shape, 1)
    row = lax.broadcasted_iota(jnp.int32, z.shape, 0)
    log_f = -_softplus(-z)
    beta = _sigmoid(z)
    g = -jnp.exp(p_ref[1:2, :]) * _softplus(z)
    cum_f = _scan_rows(log_f, row, tb) + carry_ref[0:1, :]
    carry_ref[0:1, :] = cum_f[tb - 1:tb, :]
    cum_g = _scan_rows(g, row, GDN_CHUNK)
    out = jnp.where(lane < LANE_BETA, cum_f, jnp.where(lane < LANE_G, beta, cum_g))
    g_ref[...] = out
    gt_ref[...] = out.T[:GATE_ROWS, :]


def _gates(small, bias_row, alog_row, batch, seq, tb=512):
    m = small.shape[0]
    nt = seq // tb
    params = jnp.zeros((SUBLANES, LANES), F32).at[0].set(bias_row).at[1].set(alog_row)
    est = 2 * tb * LANES * 4 * 3 + 16 * tb * LANES * 4
    return pl.pallas_call(
        _gates_kernel,
        grid=(batch, nt),
        in_specs=[pl.BlockSpec((tb, LANES), lambda b, t: (b * nt + t, 0)),
                  pl.BlockSpec((SUBLANES, LANES), lambda b, t: (0, 0))],
        out_specs=[pl.BlockSpec((tb, LANES), lambda b, t: (b * nt + t, 0)),
                   pl.BlockSpec((GATE_ROWS, tb), lambda b, t: (0, b * nt + t))],
        out_shape=[jax.ShapeDtypeStruct((m, LANES), F32),
                   jax.ShapeDtypeStruct((GATE_ROWS, m), F32)],
        scratch_shapes=[pltpu.VMEM((SUBLANES, LANES), F32)],
        compiler_params=_params(("arbitrary", "arbitrary"), est),
        name="gates",
    )(small, params)


def _pool_kernel(x_ref, w_ref, sc_ref, o_ref, tail_ref):
    t = pl.program_id(1)

    @pl.when(t == 0)
    def _():
        tail_ref[...] = jnp.zeros_like(tail_ref)

    tb = x_ref.shape[0]
    halo = tail_ref.shape[0]
    cg = POOL_GROUP_DIM
    x = x_ref[...].astype(F32)
    xe = jnp.concatenate([tail_ref[...], x], axis=0)
    tail_ref[...] = x[tb - halo:, :]
    pos = (t * tb + 1 + lax.broadcasted_iota(jnp.int32, (tb, cg), 0)).astype(F32)
    for gi, win in enumerate(POOL_WINDOWS):
        s = xe[:, gi * cg:(gi + 1) * cg]
        span = 1
        while span < win:
            s = s + pltpu.roll(s, span, axis=0)
            span *= 2
        mean = s[halo:, :] / jnp.minimum(pos, float(win))
        pooled = (mean - x[:, gi * cg:(gi + 1) * cg]).astype(BF16)
        y = jnp.dot(pooled, w_ref[gi], preferred_element_type=F32)
        o_ref[:, gi * cg:(gi + 1) * cg] = (y * sc_ref[:, gi * cg:(gi + 1) * cg]).astype(o_ref.dtype)


def _pool(big, pool_w, pool_scale, layer, batch, seq, tb=512):
    m = big.shape[0]
    nt = seq // tb
    halo = 16
    assert halo >= max(POOL_WINDOWS)
    est = 2 * tb * POOL_DIM * 4 + 8 * tb * POOL_DIM * 4
    return pl.pallas_call(
        _pool_kernel,
        grid=(batch, nt),
        in_specs=[pl.BlockSpec((tb, POOL_DIM), lambda b, t: (b * nt + t, 0)),
                  pl.BlockSpec((None, POOL_GROUPS, POOL_GROUP_DIM, POOL_GROUP_DIM), lambda b, t: (layer, 0, 0, 0)),
                  pl.BlockSpec((None, 1, POOL_DIM), lambda b, t: (layer, 0, 0))],
        out_specs=pl.BlockSpec((tb, POOL_DIM), lambda b, t: (b * nt + t, 0)),
        out_shape=jax.ShapeDtypeStruct((m, D_MODEL), BF16),
        scratch_shapes=[pltpu.VMEM((halo, POOL_DIM), F32)],
        compiler_params=_params(("arbitrary", "arbitrary"), est),
        name="pool_mixer",
    )(big, pool_w, pool_scale.reshape(pool_scale.shape[0], 1, POOL_DIM))


def _fox_kernel(q_ref, k_ref, v_ref, g_ref, cr_ref, mix_ref, o_ref, *, tq):
    del mix_ref
    h = pl.program_id(1)
    qi = pl.program_id(2)
    log2e = math.log2(math.e)
    scale2 = log2e / math.sqrt(HEAD_DIM)
    q = q_ref[...]
    gates = g_ref[...]
    lane = lax.broadcasted_iota(jnp.int32, gates.shape, 1)
    cq = jnp.sum(jnp.where(lane == LANE_F + h, gates, 0.0), axis=1, keepdims=True) * log2e
    tri = (lax.broadcasted_iota(jnp.int32, (tq, tq), 0) >= lax.broadcasted_iota(jnp.int32, (tq, tq), 1))

    def step(ki, carry, masked):
        m_prev, l_prev, acc = carry
        start = pl.multiple_of(ki * tq, tq)
        kb = k_ref[pl.ds(start, tq), :]
        vb = v_ref[pl.ds(start, tq), :]
        s = lax.dot_general(q, kb, NT_DIMS, preferred_element_type=F32)
        zp = s * scale2 - cr_ref[ki] * log2e
        if masked:
            zp = jnp.where(tri, zp, -jnp.inf)
        m_new = jnp.maximum(m_prev, jnp.max(zp, axis=1, keepdims=True) + cq)
        p = jnp.exp2(zp - (m_new - cq))
        alpha = jnp.exp2(m_prev - m_new)
        l_new = alpha * l_prev + jnp.sum(p, axis=1, keepdims=True)
        acc = alpha * acc + jnp.dot(p.astype(BF16), vb, preferred_element_type=F32)
        return m_new, l_new, acc

    init = (jnp.full((tq, 1), -jnp.inf, F32), jnp.zeros((tq, 1), F32), jnp.zeros((tq, HEAD_DIM), F32))
    carry = lax.fori_loop(0, qi, lambda ki, c: step(ki, c, False), init)
    _, l_fin, acc = step(qi, carry, True)
    o_ref[...] = (acc / l_fin).astype(o_ref.dtype)


def _fox_attention(big, gates, gates_t, mix, batch, seq, tq=512):
    m = big.shape[0]
    nq = seq // tq
    cr = gates_t.reshape(GATE_ROWS, m // tq, 1, tq)
    est = 2 * (tq * HEAD_DIM * 2 * 2 + 2 * seq * HEAD_DIM * 2 + tq * LANES * 4 + seq * 4 * 8) + 10 * tq * tq * 4
    return pl.pallas_call(
        functools.partial(_fox_kernel, tq=tq),
        grid=(batch, FOX_HEADS, nq),
        in_specs=[
            pl.BlockSpec((tq, HEAD_DIM), lambda b, h, i: (b * nq + i, FOX_COL0 + h)),
            pl.BlockSpec((seq, HEAD_DIM), lambda b, h, i: (b, FOX_COL0 + FOX_HEADS + h)),
            pl.BlockSpec((seq, HEAD_DIM), lambda b, h, i: (b, FOX_COL0 + 2 * FOX_HEADS + h)),
            pl.BlockSpec((tq, LANES), lambda b, h, i: (b * nq + i, 0)),
            pl.BlockSpec((None, nq, 1, tq), lambda b, h, i: (LANE_F + h, b, 0, 0)),
            pl.BlockSpec(memory_space=pl.ANY),
        ],
        out_specs=pl.BlockSpec((tq, HEAD_DIM), lambda b, h, i: (b * nq + i, MIX_FOX_COL0 + h)),
        out_shape=jax.ShapeDtypeStruct(mix.shape, mix.dtype),
        input_output_aliases={5: 0},
        compiler_params=_params(("arbitrary", "arbitrary", "arbitrary"), est),
        name="fox_attention",
    )(big, big, big, gates, cr, mix)


def _bdot(a, b):
    return jnp.dot(a.astype(BF16), b.astype(BF16), preferred_element_type=F32)


def _inv_unit_lower(lows, row, col):
    n = lows[0].shape[0]
    eye = (row == col).astype(F32)
    diag = (row >> 3) == (col >> 3)
    ds = [jnp.where(diag, low, 0.0) for low in lows]
    d2s = [_bdot(d, d) for d in ds]
    d4s = [_bdot(d2, d2) for d2 in d2s]
    invs = [eye - d for d in ds]
    invs = [inv + _bdot(inv, d2) for inv, d2 in zip(invs, d2s)]
    invs = [inv + _bdot(inv, d4) for inv, d4 in zip(invs, d4s)]
    shift = 3
    while (1 << shift) < n:
        rb = row >> shift
        cb = col >> shift
        join = ((rb & 1) == 1) & (cb == rb - 1)
        inv16s = [inv.astype(BF16) for inv in invs]
        xs = [_bdot(jnp.where(join, low, 0.0), inv16) for low, inv16 in zip(lows, inv16s)]
        invs = [inv - _bdot(inv16, x) for inv, inv16, x in zip(invs, inv16s, xs)]
        shift += 1
    return invs


def _gdn_kernel(q_ref, k_ref, v_ref, z_ref, g_ref, gr_ref, cwq_ref, cwk_ref, cwv_ref, gain_ref, mix_ref,
                o_ref, state_ref, tail_ref, *, heads):
    del mix_ref
    hg = pl.program_id(1)
    t = pl.program_id(2)

    @pl.when(t == 0)
    def _():
        state_ref[...] = jnp.zeros_like(state_ref)
        tail_ref[...] = jnp.zeros_like(tail_ref)

    tb = q_ref.shape[0]
    c = GDN_CHUNK
    dk = HEAD_DIM

    def conv_silu(x_ref, w_ref, slot):
        assert GDN_CONV == 4
        x = x_ref[...].astype(F32)
        xe = jnp.concatenate([tail_ref[slot], x], axis=0)
        x1 = pltpu.roll(xe, 1, axis=0)
        w = w_ref[...]
        near = x * w[3:4, :] + x1[SUBLANES:, :] * w[2:3, :]
        far = xe * w[1:2, :] + x1 * w[0:1, :]
        y = near + pltpu.roll(far, 2, axis=0)[SUBLANES:, :]
        tail_ref[slot] = x[tb - SUBLANES:, :]
        return y * _sigmoid(y)

    def l2n(x):
        return x * lax.rsqrt(jnp.sum(x * x, axis=-1, keepdims=True) + EPS)

    q_raw = conv_silu(q_ref, cwq_ref, 0)
    k_raw = conv_silu(k_ref, cwk_ref, 1)
    v_raw = conv_silu(v_ref, cwv_ref, 2)
    gates = g_ref[...]
    lane = lax.broadcasted_iota(jnp.int32, gates.shape, 1)

    row = lax.broadcasted_iota(jnp.int32, (c, c), 0)
    col = lax.broadcasted_iota(jnp.int32, (c, c), 1)
    causal = row >= col
    strict = row > col
    last_lane = lax.broadcasted_iota(jnp.int32, (1, c), 1) == c - 1
    chunks = [slice(ci * c, (ci + 1) * c) for ci in range(tb // c)]

    hd = []
    for g in range(heads):
        cols = slice(g * dk, (g + 1) * dk)
        head = hg * heads + g
        q = l2n(q_raw[:, cols]) * (dk ** -0.5)
        k = l2n(k_raw[:, cols])
        beta = jnp.sum(jnp.where(lane == LANE_BETA + head, gates, 0.0), axis=1, keepdims=True)
        gcol = jnp.sum(jnp.where(lane == LANE_G + head, gates, 0.0), axis=1, keepdims=True)
        exp_g = jnp.exp(gcol)
        k_beta = k * beta
        hd.append(dict(cols=cols, q16=q.astype(BF16), k=k, k16=k.astype(BF16), kb16=k_beta.astype(BF16),
                       gcol=gcol, grow=gr_ref[g],
                       rhs=jnp.concatenate([v_raw[:, cols] * beta, k_beta * exp_g], axis=1).astype(BF16),
                       q_dec=q * exp_g))

    items = [(g, sl) for sl in chunks for g in range(heads)]
    decays = [jnp.exp(jnp.where(causal, hd[g]["gcol"][sl] - hd[g]["grow"][:, sl], -jnp.inf)) for g, sl in items]
    a_mats = [jnp.where(strict, lax.dot_general(hd[g]["kb16"][sl], hd[g]["k16"][sl], NT_DIMS,
                                                preferred_element_type=F32) * dec, 0.0)
              for (g, sl), dec in zip(items, decays)]
    intras = [jnp.where(causal, lax.dot_general(hd[g]["q16"][sl], hd[g]["k16"][sl], NT_DIMS,
                                                preferred_element_type=F32) * dec, 0.0).astype(BF16)
              for (g, sl), dec in zip(items, decays)]
    invs = _inv_unit_lower(a_mats, row, col)
    sols = [jnp.dot(inv.astype(BF16), hd[g]["rhs"][sl], preferred_element_type=F32)
            for (g, sl), inv in zip(items, invs)]

    states = [state_ref[g] for g in range(heads)]
    for ci, sl in enumerate(chunks):
        base = ci * heads
        g_lasts = [jnp.sum(jnp.where(last_lane, hd[g]["grow"][:, sl], 0.0), axis=1, keepdims=True)
                   for g in range(heads)]
        k_decs = [(hd[g]["k"][sl] * jnp.exp(g_lasts[g] - hd[g]["gcol"][sl])).astype(BF16) for g in range(heads)]
        wqs = [jnp.concatenate([sols[base + g][:, dk:], hd[g]["q_dec"][sl]], axis=0).astype(BF16)
               for g in range(heads)]
        wss = [jnp.dot(wqs[g], states[g].astype(BF16), preferred_element_type=F32) for g in range(heads)]
        vns = [(sols[base + g][:, :dk] - wss[g][:c]).astype(BF16) for g in range(heads)]
        outs = [wss[g][c:] + jnp.dot(intras[base + g], vns[g], preferred_element_type=F32) for g in range(heads)]
        states = [states[g] * jnp.exp(g_lasts[g])
                  + lax.dot_general(k_decs[g], vns[g], TN_DIMS, preferred_element_type=F32) for g in range(heads)]
        for g in range(heads):
            cols = hd[g]["cols"]
            z = z_ref[sl, cols].astype(F32)
            o = outs[g]
            o = o * lax.rsqrt(jnp.mean(o * o, axis=-1, keepdims=True) + EPS) * gain_ref[...]
            o_ref[sl, cols] = (o * (z * _sigmoid(z))).astype(o_ref.dtype)
    for g in range(heads):
        state_ref[g] = states[g]


def _gdn(big, gates, gates_t, conv_w, norm_gain, mix, layer, batch, seq, tb=512, heads=4):
    m = big.shape[0]
    nt = seq // tb
    width = heads * HEAD_DIM
    assert GDN_HEADS % heads == 0 and all(off % heads == 0 for off in (GDN_COL0, Z_COL0, MIX_GDN_COL0, LANE_G))
    gr = gates_t.reshape(GATE_ROWS, 1, m)
    est = 2 * (5 * tb * width * 2 + tb * LANES * 4 + heads * tb * 32) + 64 * tb * width * 4
    blk = lambda off: pl.BlockSpec((tb, width), lambda b, h, t: (b * nt + t, off // heads + h))
    cw = lambda off: pl.BlockSpec((None, GDN_CONV, width), lambda b, h, t: (layer, 0, off // heads + h))
    return pl.pallas_call(
        functools.partial(_gdn_kernel, heads=heads),
        grid=(batch, GDN_HEADS // heads, nt),
        in_specs=[
            blk(GDN_COL0), blk(GDN_COL0 + GDN_HEADS), blk(GDN_COL0 + 2 * GDN_HEADS), blk(Z_COL0),
            pl.BlockSpec((tb, LANES), lambda b, h, t: (b * nt + t, 0)),
            pl.BlockSpec((heads, 1, tb), lambda b, h, t: (LANE_G // heads + h, 0, b * nt + t)),
            cw(0), cw(GDN_HEADS), cw(2 * GDN_HEADS),
            pl.BlockSpec((None, 1, HEAD_DIM), lambda b, h, t: (layer, 0, 0)),
            pl.BlockSpec(memory_space=pl.ANY),
        ],
        out_specs=pl.BlockSpec((tb, width), lambda b, h, t: (b * nt + t, MIX_GDN_COL0 // heads + h)),
        out_shape=jax.ShapeDtypeStruct(mix.shape, mix.dtype),
        input_output_aliases={10: 0},
        scratch_shapes=[pltpu.VMEM((heads, HEAD_DIM, HEAD_DIM), F32),
                        pltpu.VMEM((3, SUBLANES, width), F32)],
        compiler_params=_params(("arbitrary", "arbitrary", "arbitrary"), est),
        name="gated_delta_rule",
    )(big, big, big, big, gates, gr, conv_w, conv_w, conv_w,
      norm_gain.reshape(norm_gain.shape[0], 1, HEAD_DIM), mix)


def _ffn_up_kernel(h_ref, wg_ref, wu_ref, cg_ref, cu_ref, o_ref, tail_ref, *, blocks_per_seq):
    i = pl.program_id(0)
    j = pl.program_id(1)
    tm = h_ref.shape[0]

    @pl.when(i % blocks_per_seq == 0)
    def _():
        tail_ref[j] = jnp.zeros(tail_ref.shape[1:], F32)

    h = h_ref[...]
    yg = jnp.dot(h, wg_ref[...], preferred_element_type=F32)
    yu = jnp.dot(h, wu_ref[...], preferred_element_type=F32)
    tail = tail_ref[j]

    def conv(y, prev, w):
        ye = jnp.concatenate([prev, y], axis=0)
        out = y * w[FFN_CONV - 1:FFN_CONV, :]
        for back in range(1, FFN_CONV):
            out = out + pltpu.roll(ye, back, axis=0)[SUBLANES:, :] * w[FFN_CONV - 1 - back:FFN_CONV - back, :]
        return out

    ug = conv(yg, tail[:SUBLANES], cg_ref[...])
    uu = conv(yu, tail[SUBLANES:], cu_ref[...])
    tail_ref[j] = jnp.concatenate([yg[tm - SUBLANES:], yu[tm - SUBLANES:]], axis=0)
    o_ref[...] = (ug * _sigmoid(ug) * uu).astype(o_ref.dtype)


def _ffn_up(h, w_up, conv_w, layer, seq, tm=1024, tn=256):
    m, d = h.shape
    nj = FFN_DIM // tn
    est = 2 * (tm * d * 2 + 2 * d * tn * 2 + tm * tn * 2) + nj * 2 * SUBLANES * tn * 4 + 12 * tm * tn * 4
    return pl.pallas_call(
        functools.partial(_ffn_up_kernel, blocks_per_seq=seq // tm),
        grid=(m // tm, nj),
        in_specs=[pl.BlockSpec((tm, d), lambda i, j: (i, 0)),
                  pl.BlockSpec((None, d, tn), lambda i, j: (layer, 0, j)),
                  pl.BlockSpec((None, d, tn), lambda i, j: (layer, 0, nj + j)),
                  pl.BlockSpec((None, FFN_CONV, tn), lambda i, j: (layer, 0, j)),
                  pl.BlockSpec((None, FFN_CONV, tn), lambda i, j: (layer, 0, nj + j))],
        out_specs=pl.BlockSpec((tm, tn), lambda i, j: (i, j)),
        out_shape=jax.ShapeDtypeStruct((m, FFN_DIM), BF16),
        scratch_shapes=[pltpu.VMEM((nj, 2 * SUBLANES, tn), F32)],
        compiler_params=_params(("arbitrary", "arbitrary"), est),
        name="ffn_up_conv_gate",
    )(h, w_up, w_up, conv_w, conv_w)


def kernel(x, norm_mix_gain, w_in, pool_w, pool_scale, fox_f_bias, gdn_conv_w, gdn_A_log, gdn_dt_bias,
           gdn_norm_gain, w_o, norm_ffn_gain, w_up, ffn_conv_w, w_down, final_norm_gain):
    batch, seq, d = x.shape
    n_layers = norm_mix_gain.shape[0]
    assert d == D_MODEL and w_in.shape[2] == IN_DIM and seq % 1024 == 0
    w_big, w_small = _regroup_w_in(w_in)
    w_o16, w_up16, w_down16, pool_w16 = (w.astype(BF16) for w in (w_o, w_up, w_down, pool_w))
    zeros_h = jnp.zeros((n_layers, GDN_HEADS), F32)
    zeros_pad = jnp.zeros((n_layers, LANES - N_GATES), F32)
    gate_bias = jnp.concatenate([fox_f_bias.astype(F32), zeros_h, gdn_dt_bias.astype(F32), zeros_pad], axis=1)
    gate_alog = jnp.concatenate([jnp.zeros((n_layers, FOX_HEADS), F32), zeros_h, gdn_A_log.astype(F32), zeros_pad],
                                axis=1)

    xf = x.reshape(batch * seq, d).astype(F32)
    for l in range(n_layers):
        h = _rmsnorm(xf, norm_mix_gain[l], BF16)
        big = _matmul(h, w_big, l, BF16, tm=1024, tn=512, name="in_proj")
        small = _matmul(h, w_small, l, F32, tm=1024, tn=LANES, name="gate_proj")
        gates, gates_t = _gates(small, gate_bias[l], gate_alog[l], batch, seq)
        mix = _pool(big, pool_w16, pool_scale, l, batch, seq)
        mix = _fox_attention(big, gates, gates_t, mix, batch, seq)
        mix = _gdn(big, gates, gates_t, gdn_conv_w, gdn_norm_gain, mix, l, batch, seq)
        xf = _matmul(mix, w_o16, l, F32, tm=1024, tn=512, residual=xf, name="out_proj")
        h = _rmsnorm(xf, norm_ffn_gain[l], BF16)
        act = _ffn_up(h, w_up16, ffn_conv_w, l, seq)
        xf = _matmul(act, w_down16, l, F32, tm=1024, tn=512, residual=xf, k_steps=2, name="ffn_down")
    out = _rmsnorm(xf, final_norm_gain, x.dtype)
    return out.reshape(batch, seq, d)
```

```python
import functools
import math

import jax
import jax.numpy as jnp
from jax import lax
from jax.experimental import pallas as pl
from jax.experimental.pallas import tpu as pltpu

D_MODEL = 4096
HEAD_DIM = 128
POOL_WINDOWS = (2, 4, 8, 16)
POOL_GROUPS = 4
POOL_GROUP_DIM = D_MODEL // 16
POOL_DIM = POOL_GROUPS * POOL_GROUP_DIM
ATTN_DIM = (D_MODEL - POOL_DIM) // 2
FOX_HEADS = ATTN_DIM // HEAD_DIM
GDN_DIM = D_MODEL - POOL_DIM - ATTN_DIM
GDN_HEADS = GDN_DIM // HEAD_DIM
GDN_CONV = 4
FFN_DIM = 11008
FFN_CONV = 3
EPS = 1e-6
IN_DIM = POOL_DIM + 3 * ATTN_DIM + FOX_HEADS + 3 * GDN_DIM + GDN_DIM + 2 * GDN_HEADS

LANES = 128
SUBLANES = 8
VMEM_BYTES_V7X = 64 * 1024 * 1024
VMEM_CAP = VMEM_BYTES_V7X - 8 * 1024 * 1024

ALIGNED_DIM = POOL_DIM + 3 * ATTN_DIM
BIG_DIM = ALIGNED_DIM + 3 * GDN_DIM + GDN_DIM
N_GATES = FOX_HEADS + 2 * GDN_HEADS
TAIL_DIM = IN_DIM - BIG_DIM
FOX_COL0 = POOL_DIM // LANES
GDN_COL0 = ALIGNED_DIM // LANES
Z_COL0 = (ALIGNED_DIM + 3 * GDN_DIM) // LANES
MIX_FOX_COL0 = POOL_DIM // LANES
MIX_GDN_COL0 = (POOL_DIM + ATTN_DIM) // LANES
LANE_F = 0
LANE_BETA = FOX_HEADS
LANE_G = FOX_HEADS + GDN_HEADS
GATE_ROWS = 48
GDN_CHUNK = 128

F32 = jnp.float32
BF16 = jnp.bfloat16
NT_DIMS = (((1,), (1,)), ((), ()))
TN_DIMS = (((0,), (0,)), ((), ()))


def _params(semantics, vmem_estimate):
    limit = min(int(vmem_estimate * 1.25) + (4 << 20), VMEM_CAP)
    return pltpu.CompilerParams(dimension_semantics=semantics, vmem_limit_bytes=limit)


def _sigmoid(x):
    return 1.0 / (1.0 + jnp.exp(-x))


def _softplus(x):
    return jnp.maximum(x, 0.0) + jnp.log1p(jnp.exp(-jnp.abs(x)))


def _shift_rows(x, prev, k):
    n, w = x.shape
    x3 = jnp.concatenate([prev, x], axis=0).reshape(n // SUBLANES + 1, SUBLANES, w)
    r = pltpu.roll(x3, k, axis=1)
    sub = lax.broadcasted_iota(jnp.int32, (n // SUBLANES, SUBLANES, w), 1)
    return jnp.where(sub < k, r[:-1], r[1:]).reshape(n, w)


def _regroup_kernel(main_ref, tail_ref, big_ref, small_ref):
    tk = main_ref.shape[0]
    big_ref[:, :ALIGNED_DIM] = main_ref[:, :ALIGNED_DIM].astype(BF16)
    r = lax.broadcasted_iota(jnp.int32, (2 * LANES, LANES), 0)
    c = lax.broadcasted_iota(jnp.int32, (2 * LANES, LANES), 1)
    shift_mat = jnp.where(r == c + FOX_HEADS, 1.0, 0.0).astype(BF16)
    for t in range((BIG_DIM - ALIGNED_DIM) // LANES):
        src = ALIGNED_DIM + t * LANES
        if src + 2 * LANES <= BIG_DIM:
            win = main_ref[:, src:src + 2 * LANES]
        else:
            win = jnp.concatenate([main_ref[:, src:src + LANES], tail_ref[...]], axis=1)
        moved = jnp.dot(win.astype(BF16), shift_mat, preferred_element_type=F32)
        big_ref[:, src:src + LANES] = moved.astype(BF16)
    lane = lax.broadcasted_iota(jnp.int32, (tk, LANES), 1)
    gate_cols = jnp.where(lane < FOX_HEADS, main_ref[:, ALIGNED_DIM:ALIGNED_DIM + LANES], tail_ref[...])
    small_ref[...] = jnp.where(lane < N_GATES, gate_cols, 0.0).astype(BF16)


def _regroup_w_in(w_in, tk=256):
    n_layers, d, _ = w_in.shape
    tail = jnp.pad(w_in[:, :, BIG_DIM:], ((0, 0), (0, 0), (0, LANES - TAIL_DIM)))
    est = 2 * tk * BIG_DIM * (4 + 2) + 4 * tk * LANES * 4
    return pl.pallas_call(
        _regroup_kernel,
        grid=(n_layers, d // tk),
        in_specs=[pl.BlockSpec((None, tk, BIG_DIM), lambda l, i: (l, i, 0)),
                  pl.BlockSpec((None, tk, LANES), lambda l, i: (l, i, 0))],
        out_specs=[pl.BlockSpec((None, tk, BIG_DIM), lambda l, i: (l, i, 0)),
                   pl.BlockSpec((None, tk, LANES), lambda l, i: (l, i, 0))],
        out_shape=[jax.ShapeDtypeStruct((n_layers, d, BIG_DIM), BF16),
                   jax.ShapeDtypeStruct((n_layers, d, LANES), BF16)],
        compiler_params=_params(("arbitrary", "arbitrary"), est),
        name="regroup_w_in",
    )(w_in, tail)


def _rms_kernel(x_ref, g_ref, o_ref):
    x = x_ref[...]
    ms = jnp.mean(x * x, axis=-1, keepdims=True)
    o_ref[...] = (x * lax.rsqrt(ms + EPS) * g_ref[...]).astype(o_ref.dtype)


def _rmsnorm(x, gain, out_dtype, tm=512):
    m, d = x.shape
    est = 2 * tm * d * (4 + jnp.dtype(out_dtype).itemsize)
    return pl.pallas_call(
        _rms_kernel,
        grid=(m // tm,),
        in_specs=[pl.BlockSpec((tm, d), lambda i: (i, 0)),
                  pl.BlockSpec((1, d), lambda i: (0, 0))],
        out_specs=pl.BlockSpec((tm, d), lambda i: (i, 0)),
        out_shape=jax.ShapeDtypeStruct((m, d), out_dtype),
        compiler_params=_params(("arbitrary",), est),
        name="rmsnorm",
    )(x, gain.reshape(1, d))


def _mm_kernel(*refs, has_res, k_steps):
    a_ref, b_ref = refs[0], refs[1]
    res_ref = refs[2] if has_res else None
    o_ref = refs[-1]
    acc = jnp.dot(a_ref[...], b_ref[...], preferred_element_type=F32)
    if k_steps == 1:
        if has_res:
            acc = acc + res_ref[...]
        o_ref[...] = acc.astype(o_ref.dtype)
    else:
        k = pl.program_id(2)

        @pl.when(k == 0)
        def _():
            o_ref[...] = (acc + res_ref[...]) if has_res else acc

        @pl.when(k > 0)
        def _():
            o_ref[...] += acc


def _matmul(a, w, layer, out_dtype, tm, tn, residual=None, k_steps=1, name="matmul"):
    m, kdim = a.shape
    n = w.shape[2]
    tk = kdim // k_steps
    has_res = residual is not None
    assert k_steps == 1 or out_dtype == F32
    in_specs = [pl.BlockSpec((tm, tk), lambda i, j, k: (i, k)),
                pl.BlockSpec((None, tk, tn), lambda i, j, k: (layer, k, j))]
    args = [a, w]
    est = 2 * tm * tk * a.dtype.itemsize + 2 * tk * tn * w.dtype.itemsize
    if has_res:
        in_specs.append(pl.BlockSpec((tm, tn), lambda i, j, k: (i, j)))
        est += 2 * tm * tn * 4
        args.append(residual)
    est += 2 * tm * tn * jnp.dtype(out_dtype).itemsize + 2 * tm * tn * 4
    return pl.pallas_call(
        functools.partial(_mm_kernel, has_res=has_res, k_steps=k_steps),
        grid=(m // tm, n // tn, k_steps),
        in_specs=in_specs,
        out_specs=pl.BlockSpec((tm, tn), lambda i, j, k: (i, j)),
        out_shape=jax.ShapeDtypeStruct((m, n), out_dtype),
        compiler_params=_params(("arbitrary", "arbitrary", "arbitrary"), est),
        name=name,
    )(*args)


def _scan_rows(y, row, seg):
    pos = row & (seg - 1)
    s = 1
    while s < seg:
        y = y + jnp.where(pos >= s, pltpu.roll(y, s, axis=0), 0.0)
        s *= 2
    return y


def _gates_kernel(x_ref, p_ref, g_ref, gt_ref, carry_ref):
    t = pl.program_id(1)

    @pl.when(t == 0)
    def _():
        carry_ref[...] = jnp.zeros_like(carry_ref)

    tb = x_ref.shape[0]
    z = x_ref[...] + p_ref[0:1, :]
    lane = lax.broadcasted_iota(jnp.int32, z.shape, 1)
    row = lax.broadcasted_iota(jnp.int32, z.shape, 0)
    log_f = -_softplus(-z)
    beta = _sigmoid(z)
    g = -jnp.exp(p_ref[1:2, :]) * _softplus(z)
    cum_f = _scan_rows(log_f, row, tb) + carry_ref[0:1, :]
    carry_ref[0:1, :] = cum_f[tb - 1:tb, :]
    cum_g = _scan_rows(g, row, GDN_CHUNK)
    out = jnp.where(lane < LANE_BETA, cum_f, jnp.where(lane < LANE_G, beta, cum_g))
    g_ref[...] = out
    gt_ref[...] = out.T[:GATE_ROWS, :]


def _gates(small, bias_row, alog_row, batch, seq, tb=512):
    m = small.shape[0]
    nt = seq // tb
    params = jnp.zeros((SUBLANES, LANES), F32).at[0].set(bias_row).at[1].set(alog_row)
    est = 2 * tb * LANES * 4 * 3 + 16 * tb * LANES * 4
    return pl.pallas_call(
        _gates_kernel,
        grid=(batch, nt),
        in_specs=[pl.BlockSpec((tb, LANES), lambda b, t: (b * nt + t, 0)),
                  pl.BlockSpec((SUBLANES, LANES), lambda b, t: (0, 0))],
        out_specs=[pl.BlockSpec((tb, LANES), lambda b, t: (b * nt + t, 0)),
                   pl.BlockSpec((GATE_ROWS, tb), lambda b, t: (0, b * nt + t))],
        out_shape=[jax.ShapeDtypeStruct((m, LANES), F32),
                   jax.ShapeDtypeStruct((GATE_ROWS, m), F32)],
        scratch_shapes=[pltpu.VMEM((SUBLANES, LANES), F32)],
        compiler_params=_params(("arbitrary", "arbitrary"), est),
        name="gates",
    )(small, params)


def _pool_kernel(x_ref, w_ref, sc_ref, o_ref, tail_ref):
    t = pl.program_id(1)

    @pl.when(t == 0)
    def _():
        tail_ref[...] = jnp.zeros_like(tail_ref)

    tb = x_ref.shape[0]
    halo = tail_ref.shape[0]
    cg = POOL_GROUP_DIM
    x = x_ref[...].astype(F32)
    xe = jnp.concatenate([tail_ref[...], x], axis=0)
    tail_ref[...] = x[tb - halo:, :]
    pos = (t * tb + 1 + lax.broadcasted_iota(jnp.int32, (tb, cg), 0)).astype(F32)
    for gi, win in enumerate(POOL_WINDOWS):
        s = xe[:, gi * cg:(gi + 1) * cg]
        span = 1
        while span < win:
            s = s + pltpu.roll(s, span, axis=0)
            span *= 2
        mean = s[halo:, :] / jnp.minimum(pos, float(win))
        pooled = (mean - x[:, gi * cg:(gi + 1) * cg]).astype(BF16)
        y = jnp.dot(pooled, w_ref[gi], preferred_element_type=F32)
        o_ref[:, gi * cg:(gi + 1) * cg] = (y * sc_ref[:, gi * cg:(gi + 1) * cg]).astype(o_ref.dtype)


def _pool(big, pool_w, pool_scale, layer, batch, seq, tb=512):
    m = big.shape[0]
    nt = seq // tb
    halo = 16
    assert halo >= max(POOL_WINDOWS)
    est = 2 * tb * POOL_DIM * 4 + 8 * tb * POOL_DIM * 4
    return pl.pallas_call(
        _pool_kernel,
        grid=(batch, nt),
        in_specs=[pl.BlockSpec((tb, POOL_DIM), lambda b, t: (b * nt + t, 0)),
                  pl.BlockSpec((None, POOL_GROUPS, POOL_GROUP_DIM, POOL_GROUP_DIM), lambda b, t: (layer, 0, 0, 0)),
                  pl.BlockSpec((None, 1, POOL_DIM), lambda b, t: (layer, 0, 0))],
        out_specs=pl.BlockSpec((tb, POOL_DIM), lambda b, t: (b * nt + t, 0)),
        out_shape=jax.ShapeDtypeStruct((m, D_MODEL), BF16),
        scratch_shapes=[pltpu.VMEM((halo, POOL_DIM), F32)],
        compiler_params=_params(("arbitrary", "arbitrary"), est),
        name="pool_mixer",
    )(big, pool_w, pool_scale.reshape(pool_scale.shape[0], 1, POOL_DIM))


def _fox_kernel(q_ref, k_ref, v_ref, g_ref, cr_ref, mix_ref, o_ref, *, tq, heads):
    del mix_ref
    hg = pl.program_id(1)
    qi = pl.program_id(2)
    log2e = math.log2(math.e)
    scale2 = log2e / math.sqrt(HEAD_DIM)
    dh = HEAD_DIM
    gates = g_ref[...]
    lane = lax.broadcasted_iota(jnp.int32, gates.shape, 1)
    qs = [q_ref[:, g * dh:(g + 1) * dh] for g in range(heads)]
    cqs = [jnp.sum(jnp.where(lane == LANE_F + hg * heads + g, gates, 0.0), axis=1, keepdims=True) * log2e
           for g in range(heads)]
    tri = (lax.broadcasted_iota(jnp.int32, (tq, tq), 0) >= lax.broadcasted_iota(jnp.int32, (tq, tq), 1))

    def step(ki, carry, masked):
        start = pl.multiple_of(ki * tq, tq)
        kbs = [k_ref[pl.ds(start, tq), g * dh:(g + 1) * dh] for g in range(heads)]
        vbs = [v_ref[pl.ds(start, tq), g * dh:(g + 1) * dh] for g in range(heads)]
        ss = [lax.dot_general(qs[g], kbs[g], NT_DIMS, preferred_element_type=F32) for g in range(heads)]
        zps = [ss[g] * scale2 - cr_ref[g, ki] * log2e for g in range(heads)]
        if masked:
            zps = [jnp.where(tri, zp, -jnp.inf) for zp in zps]
        m_news = [jnp.maximum(carry[g][0], jnp.max(zps[g], axis=1, keepdims=True) + cqs[g]) for g in range(heads)]
        ps = [jnp.exp2(zps[g] - (m_news[g] - cqs[g])) for g in range(heads)]
        alphas = [jnp.exp2(carry[g][0] - m_news[g]) for g in range(heads)]
        l_news = [alphas[g] * carry[g][1] + jnp.sum(ps[g], axis=1, keepdims=True) for g in range(heads)]
        accs = [alphas[g] * carry[g][2] + jnp.dot(ps[g].astype(BF16), vbs[g], preferred_element_type=F32)
                for g in range(heads)]
        return tuple((m_news[g], l_news[g], accs[g]) for g in range(heads))

    init = tuple((jnp.full((tq, 1), -jnp.inf, F32), jnp.zeros((tq, 1), F32), jnp.zeros((tq, dh), F32))
                 for _ in range(heads))
    carry = lax.fori_loop(0, qi, lambda ki, c: step(ki, c, False), init)
    final = step(qi, carry, True)
    for g in range(heads):
        _, l_fin, acc = final[g]
        o_ref[:, g * dh:(g + 1) * dh] = (acc / l_fin).astype(o_ref.dtype)


def _fox_attention(big, gates, gates_t, mix, batch, seq, tq=512, heads=2):
    m = big.shape[0]
    nq = seq // tq
    width = heads * HEAD_DIM
    assert FOX_HEADS % heads == 0 and FOX_COL0 % heads == 0 and MIX_FOX_COL0 % heads == 0 and LANE_F % heads == 0
    cr = gates_t.reshape(GATE_ROWS, m // tq, 1, tq)
    est = (2 * (tq * width * 2 * 2 + 2 * seq * width * 2 + tq * LANES * 4 + heads * seq * 4 * 8)
           + heads * 10 * tq * tq * 4)
    return pl.pallas_call(
        functools.partial(_fox_kernel, tq=tq, heads=heads),
        grid=(batch, FOX_HEADS // heads, nq),
        in_specs=[
            pl.BlockSpec((tq, width), lambda b, h, i: (b * nq + i, FOX_COL0 // heads + h)),
            pl.BlockSpec((seq, width), lambda b, h, i: (b, (FOX_COL0 + FOX_HEADS) // heads + h)),
            pl.BlockSpec((seq, width), lambda b, h, i: (b, (FOX_COL0 + 2 * FOX_HEADS) // heads + h)),
            pl.BlockSpec((tq, LANES), lambda b, h, i: (b * nq + i, 0)),
            pl.BlockSpec((heads, nq, 1, tq), lambda b, h, i: (LANE_F // heads + h, b, 0, 0)),
            pl.BlockSpec(memory_space=pl.ANY),
        ],
        out_specs=pl.BlockSpec((tq, width), lambda b, h, i: (b * nq + i, MIX_FOX_COL0 // heads + h)),
        out_shape=jax.ShapeDtypeStruct(mix.shape, mix.dtype),
        input_output_aliases={5: 0},
        compiler_params=_params(("arbitrary", "arbitrary", "arbitrary"), est),
        name="fox_attention",
    )(big, big, big, gates, cr, mix)


def _bdot(a, b):
    return jnp.dot(a.astype(BF16), b.astype(BF16), preferred_element_type=F32)


def _inv_unit_lower(lows, row, col):
    n = lows[0].shape[0]
    eye = (row == col).astype(F32)
    diag = (row >> 3) == (col >> 3)
    ds = [jnp.where(diag, low, 0.0) for low in lows]
    d2s = [_bdot(d, d) for d in ds]
    d4s = [_bdot(d2, d2) for d2 in d2s]
    invs = [eye - d for d in ds]
    invs = [inv + _bdot(inv, d2) for inv, d2 in zip(invs, d2s)]
    invs = [inv + _bdot(inv, d4) for inv, d4 in zip(invs, d4s)]
    shift = 3
    while (1 << shift) < n:
        rb = row >> shift
        cb = col >> shift
        join = ((rb & 1) == 1) & (cb == rb - 1)
        inv16s = [inv.astype(BF16) for inv in invs]
        xs = [_bdot(jnp.where(join, low, 0.0), inv16) for low, inv16 in zip(lows, inv16s)]
        invs = [inv - _bdot(inv16, x) for inv, inv16, x in zip(invs, inv16s, xs)]
        shift += 1
    return invs


def _gdn_kernel(q_ref, k_ref, v_ref, z_ref, g_ref, gr_ref, cwq_ref, cwk_ref, cwv_ref, gain_ref, mix_ref,
                o_ref, state_ref, tail_ref, *, heads):
    del mix_ref
    hg = pl.program_id(1)
    t = pl.program_id(2)

    @pl.when(t == 0)
    def _():
        state_ref[...] = jnp.zeros_like(state_ref)
        tail_ref[...] = jnp.zeros_like(tail_ref)

    tb = q_ref.shape[0]
    c = GDN_CHUNK
    dk = HEAD_DIM

    def conv_silu(x_ref, w_ref, slot):
        assert GDN_CONV == 4
        x = x_ref[...].astype(F32)
        xe = jnp.concatenate([tail_ref[slot], x], axis=0)
        x1 = pltpu.roll(xe, 1, axis=0)
        w = w_ref[...]
        near = x * w[3:4, :] + x1[SUBLANES:, :] * w[2:3, :]
        far = xe * w[1:2, :] + x1 * w[0:1, :]
        y = near + pltpu.roll(far, 2, axis=0)[SUBLANES:, :]
        tail_ref[slot] = x[tb - SUBLANES:, :]
        return y * _sigmoid(y)

    def l2n(x):
        return x * lax.rsqrt(jnp.sum(x * x, axis=-1, keepdims=True) + EPS)

    q_raw = conv_silu(q_ref, cwq_ref, 0)
    k_raw = conv_silu(k_ref, cwk_ref, 1)
    v_raw = conv_silu(v_ref, cwv_ref, 2)
    gates = g_ref[...]
    lane = lax.broadcasted_iota(jnp.int32, gates.shape, 1)

    row = lax.broadcasted_iota(jnp.int32, (c, c), 0)
    col = lax.broadcasted_iota(jnp.int32, (c, c), 1)
    causal = row >= col
    strict = row > col
    last_lane = lax.broadcasted_iota(jnp.int32, (1, c), 1) == c - 1
    chunks = [slice(ci * c, (ci + 1) * c) for ci in range(tb // c)]

    hd = []
    for g in range(heads):
        cols = slice(g * dk, (g + 1) * dk)
        head = hg * heads + g
        q = l2n(q_raw[:, cols]) * (dk ** -0.5)
        k = l2n(k_raw[:, cols])
        beta = jnp.sum(jnp.where(lane == LANE_BETA + head, gates, 0.0), axis=1, keepdims=True)
        gcol = jnp.sum(jnp.where(lane == LANE_G + head, gates, 0.0), axis=1, keepdims=True)
        exp_g = jnp.exp(gcol)
        k_beta = k * beta
        hd.append(dict(cols=cols, q16=q.astype(BF16), k=k, k16=k.astype(BF16), kb16=k_beta.astype(BF16),
                       gcol=gcol, grow=gr_ref[g],
                       rhs=jnp.concatenate([v_raw[:, cols] * beta, k_beta * exp_g], axis=1).astype(BF16),
                       q_dec=q * exp_g))

    items = [(g, sl) for sl in chunks for g in range(heads)]
    decays = [jnp.exp(jnp.where(causal, hd[g]["gcol"][sl] - hd[g]["grow"][:, sl], -jnp.inf)) for g, sl in items]
    a_mats = [jnp.where(strict, lax.dot_general(hd[g]["kb16"][sl], hd[g]["k16"][sl], NT_DIMS,
                                                preferred_element_type=F32) * dec, 0.0)
              for (g, sl), dec in zip(items, decays)]
    intras = [jnp.where(causal, lax.dot_general(hd[g]["q16"][sl], hd[g]["k16"][sl], NT_DIMS,
                                                preferred_element_type=F32) * dec, 0.0).astype(BF16)
              for (g, sl), dec in zip(items, decays)]
    invs = _inv_unit_lower(a_mats, row, col)
    sols = [jnp.dot(inv.astype(BF16), hd[g]["rhs"][sl], preferred_element_type=F32)
            for (g, sl), inv in zip(items, invs)]

    states = [state_ref[g] for g in range(heads)]
    for ci, sl in enumerate(chunks):
        base = ci * heads
        g_lasts = [jnp.sum(jnp.where(last_lane, hd[g]["grow"][:, sl], 0.0), axis=1, keepdims=True)
                   for g in range(heads)]
        k_decs = [(hd[g]["k"][sl] * jnp.exp(g_lasts[g] - hd[g]["gcol"][sl])).astype(BF16) for g in range(heads)]
        wqs = [jnp.concatenate([sols[base + g][:, dk:], hd[g]["q_dec"][sl]], axis=0).astype(BF16)
               for g in range(heads)]
        wss = [jnp.dot(wqs[g], states[g].astype(BF16), preferred_element_type=F32) for g in range(heads)]
        vns = [(sols[base + g][:, :dk] - wss[g][:c]).astype(BF16) for g in range(heads)]
        outs = [wss[g][c:] + jnp.dot(intras[base + g], vns[g], preferred_element_type=F32) for g in range(heads)]
        states = [states[g] * jnp.exp(g_lasts[g])
                  + lax.dot_general(k_decs[g], vns[g], TN_DIMS, preferred_element_type=F32) for g in range(heads)]
        for g in range(heads):
            cols = hd[g]["cols"]
            z = z_ref[sl, cols].astype(F32)
            o = outs[g]
            o = o * lax.rsqrt(jnp.mean(o * o, axis=-1, keepdims=True) + EPS) * gain_ref[...]
            o_ref[sl, cols] = (o * (z * _sigmoid(z))).astype(o_ref.dtype)
    for g in range(heads):
        state_ref[g] = states[g]


def _gdn(big, gates, gates_t, conv_w, norm_gain, mix, layer, batch, seq, tb=512, heads=4):
    m = big.shape[0]
    nt = seq // tb
    width = heads * HEAD_DIM
    assert GDN_HEADS % heads == 0 and all(off % heads == 0 for off in (GDN_COL0, Z_COL0, MIX_GDN_COL0, LANE_G))
    gr = gates_t.reshape(GATE_ROWS, 1, m)
    est = 2 * (5 * tb * width * 2 + tb * LANES * 4 + heads * tb * 32) + 64 * tb * width * 4
    blk = lambda off: pl.BlockSpec((tb, width), lambda b, h, t: (b * nt + t, off // heads + h))
    cw = lambda off: pl.BlockSpec((None, GDN_CONV, width), lambda b, h, t: (layer, 0, off // heads + h))
    return pl.pallas_call(
        functools.partial(_gdn_kernel, heads=heads),
        grid=(batch, GDN_HEADS // heads, nt),
        in_specs=[
            blk(GDN_COL0), blk(GDN_COL0 + GDN_HEADS), blk(GDN_COL0 + 2 * GDN_HEADS), blk(Z_COL0),
            pl.BlockSpec((tb, LANES), lambda b, h, t: (b * nt + t, 0)),
            pl.BlockSpec((heads, 1, tb), lambda b, h, t: (LANE_G // heads + h, 0, b * nt + t)),
            cw(0), cw(GDN_HEADS), cw(2 * GDN_HEADS),
            pl.BlockSpec((None, 1, HEAD_DIM), lambda b, h, t: (layer, 0, 0)),
            pl.BlockSpec(memory_space=pl.ANY),
        ],
        out_specs=pl.BlockSpec((tb, width), lambda b, h, t: (b * nt + t, MIX_GDN_COL0 // heads + h)),
        out_shape=jax.ShapeDtypeStruct(mix.shape, mix.dtype),
        input_output_aliases={10: 0},
        scratch_shapes=[pltpu.VMEM((heads, HEAD_DIM, HEAD_DIM), F32),
                        pltpu.VMEM((3, SUBLANES, width), F32)],
        compiler_params=_params(("arbitrary", "arbitrary", "arbitrary"), est),
        name="gated_delta_rule",
    )(big, big, big, big, gates, gr, conv_w, conv_w, conv_w,
      norm_gain.reshape(norm_gain.shape[0], 1, HEAD_DIM), mix)


def _ffn_up_kernel(h_ref, wg_ref, wu_ref, cg_ref, cu_ref, o_ref, tail_ref, *, blocks_per_seq):
    i = pl.program_id(0)
    j = pl.program_id(1)
    tm = h_ref.shape[0]

    @pl.when(i % blocks_per_seq == 0)
    def _():
        tail_ref[j] = jnp.zeros(tail_ref.shape[1:], F32)

    h = h_ref[...]
    yg = jnp.dot(h, wg_ref[...], preferred_element_type=F32)
    yu = jnp.dot(h, wu_ref[...], preferred_element_type=F32)
    tail = tail_ref[j]

    def conv(y, prev, w):
        out = y * w[FFN_CONV - 1:FFN_CONV, :]
        for back in range(1, FFN_CONV):
            out = out + _shift_rows(y, prev, back) * w[FFN_CONV - 1 - back:FFN_CONV - back, :]
        return out

    ug = conv(yg, tail[:SUBLANES], cg_ref[...])
    uu = conv(yu, tail[SUBLANES:], cu_ref[...])
    tail_ref[j] = jnp.concatenate([yg[tm - SUBLANES:], yu[tm - SUBLANES:]], axis=0)
    o_ref[...] = (ug * _sigmoid(ug) * uu).astype(o_ref.dtype)


def _ffn_up(h, w_up, conv_w, layer, seq, tm=1024, tn=256):
    m, d = h.shape
    nj = FFN_DIM // tn
    est = 2 * (tm * d * 2 + 2 * d * tn * 2 + tm * tn * 2) + nj * 2 * SUBLANES * tn * 4 + 12 * tm * tn * 4
    return pl.pallas_call(
        functools.partial(_ffn_up_kernel, blocks_per_seq=seq // tm),
        grid=(m // tm, nj),
        in_specs=[pl.BlockSpec((tm, d), lambda i, j: (i, 0)),
                  pl.BlockSpec((None, d, tn), lambda i, j: (layer, 0, j)),
                  pl.BlockSpec((None, d, tn), lambda i, j: (layer, 0, nj + j)),
                  pl.BlockSpec((None, FFN_CONV, tn), lambda i, j: (layer, 0, j)),
                  pl.BlockSpec((None, FFN_CONV, tn), lambda i, j: (layer, 0, nj + j))],
        out_specs=pl.BlockSpec((tm, tn), lambda i, j: (i, j)),
        out_shape=jax.ShapeDtypeStruct((m, FFN_DIM), BF16),
        scratch_shapes=[pltpu.VMEM((nj, 2 * SUBLANES, tn), F32)],
        compiler_params=_params(("arbitrary", "arbitrary"), est),
        name="ffn_up_conv_gate",
    )(h, w_up, w_up, conv_w, conv_w)


def kernel(x, norm_mix_gain, w_in, pool_w, pool_scale, fox_f_bias, gdn_conv_w, gdn_A_log, gdn_dt_bias,
           gdn_norm_gain, w_o, norm_ffn_gain, w_up, ffn_conv_w, w_down, final_norm_gain):
    batch, seq, d = x.shape
    n_layers = norm_mix_gain.shape[0]
    assert d == D_MODEL and w_in.shape[2] == IN_DIM and seq % 1024 == 0
    w_big, w_small = _regroup_w_in(w_in)
    w_o16, w_up16, w_down16, pool_w16 = (w.astype(BF16) for w in (w_o, w_up, w_down, pool_w))
    zeros_h = jnp.zeros((n_layers, GDN_HEADS), F32)
    zeros_pad = jnp.zeros((n_layers, LANES - N_GATES), F32)
    gate_bias = jnp.concatenate([fox_f_bias.astype(F32), zeros_h, gdn_dt_bias.astype(F32), zeros_pad], axis=1)
    gate_alog = jnp.concatenate([jnp.zeros((n_layers, FOX_HEADS), F32), zeros_h, gdn_A_log.astype(F32), zeros_pad],
                                axis=1)

    xf = x.reshape(batch * seq, d).astype(F32)
    for l in range(n_layers):
        h = _rmsnorm(xf, norm_mix_gain[l], BF16)
        big = _matmul(h, w_big, l, BF16, tm=1024, tn=512, name="in_proj")
        small = _matmul(h, w_small, l, F32, tm=1024, tn=LANES, name="gate_proj")
        gates, gates_t = _gates(small, gate_bias[l], gate_alog[l], batch, seq)
        mix = _pool(big, pool_w16, pool_scale, l, batch, seq)
        mix = _fox_attention(big, gates, gates_t, mix, batch, seq)
        mix = _gdn(big, gates, gates_t, gdn_conv_w, gdn_norm_gain, mix, l, batch, seq)
        xf = _matmul(mix, w_o16, l, F32, tm=1024, tn=512, residual=xf, name="out_proj")
        h = _rmsnorm(xf, norm_ffn_gain[l], BF16)
        act = _ffn_up(h, w_up16, ffn_conv_w, l, seq)
        xf = _matmul(act, w_down16, l, F32, tm=1024, tn=512, residual=xf, k_steps=2, name="ffn_down")
    out = _rmsnorm(xf, final_norm_gain, x.dtype)
    return out.reshape(batch, seq, d)
```

```python
import functools
import math

import jax
import jax.numpy as jnp
from jax import lax
from jax.experimental import pallas as pl
from jax.experimental.pallas import tpu as pltpu

D_MODEL = 4096
HEAD_DIM = 128
POOL_WINDOWS = (2, 4, 8, 16)
POOL_GROUPS = 4
POOL_GROUP_DIM = D_MODEL // 16
POOL_DIM = POOL_GROUPS * POOL_GROUP_DIM
ATTN_DIM = (D_MODEL - POOL_DIM) // 2
FOX_HEADS = ATTN_DIM // HEAD_DIM
GDN_DIM = D_MODEL - POOL_DIM - ATTN_DIM
GDN_HEADS = GDN_DIM // HEAD_DIM
GDN_CONV = 4
FFN_DIM = 11008
FFN_CONV = 3
EPS = 1e-6
IN_DIM = POOL_DIM + 3 * ATTN_DIM + FOX_HEADS + 3 * GDN_DIM + GDN_DIM + 2 * GDN_HEADS

LANES = 128
SUBLANES = 8
VMEM_BYTES_V7X = 64 * 1024 * 1024
VMEM_CAP = VMEM_BYTES_V7X - 8 * 1024 * 1024

ALIGNED_DIM = POOL_DIM + 3 * ATTN_DIM
BIG_DIM = ALIGNED_DIM + 3 * GDN_DIM + GDN_DIM
N_GATES = FOX_HEADS + 2 * GDN_HEADS
FOX_COL0 = POOL_DIM // LANES
GDN_COL0 = ALIGNED_DIM // LANES
Z_COL0 = (ALIGNED_DIM + 3 * GDN_DIM) // LANES
MIX_FOX_COL0 = POOL_DIM // LANES
MIX_GDN_COL0 = (POOL_DIM + ATTN_DIM) // LANES
LANE_F = 0
LANE_BETA = FOX_HEADS
LANE_G = FOX_HEADS + GDN_HEADS
GATE_ROWS = 48
GDN_CHUNK = 128

F32 = jnp.float32
BF16 = jnp.bfloat16
NT_DIMS = (((1,), (1,)), ((), ()))
TN_DIMS = (((0,), (0,)), ((), ()))


def _params(semantics, vmem_estimate):
    limit = min(int(vmem_estimate * 1.25) + (4 << 20), VMEM_CAP)
    return pltpu.CompilerParams(dimension_semantics=semantics, vmem_limit_bytes=limit)


def _sigmoid(x):
    return 1.0 / (1.0 + jnp.exp(-x))


def _softplus(x):
    return jnp.maximum(x, 0.0) + jnp.log1p(jnp.exp(-jnp.abs(x)))


def _shift_rows(x, prev, k):
    n, w = x.shape
    x3 = jnp.concatenate([prev, x], axis=0).reshape(n // SUBLANES + 1, SUBLANES, w)
    r = pltpu.roll(x3, k, axis=1)
    sub = lax.broadcasted_iota(jnp.int32, (n // SUBLANES, SUBLANES, w), 1)
    return jnp.where(sub < k, r[:-1], r[1:]).reshape(n, w)


REGROUP_COLS = 256
REGROUP_HALO = 16


def _regroup_kernel(cur_ref, nxt_ref, big_ref):
    c = pl.program_id(0)
    n_layers = cur_ref.shape[1]

    def emit(src):
        for l in range(n_layers):
            big_ref[l] = src[:, l, :].T.astype(BF16)

    @pl.when(c < ALIGNED_DIM // REGROUP_COLS)
    def _():
        emit(cur_ref[...])

    @pl.when(c >= ALIGNED_DIM // REGROUP_COLS)
    def _():
        emit(jnp.concatenate([cur_ref[FOX_HEADS:], nxt_ref[:FOX_HEADS]], axis=0))


def _gate_cols_kernel(src_ref, small_ref):
    for l in range(src_ref.shape[1]):
        small_ref[l] = src_ref[:, l, :].T.astype(BF16)


def _regroup_w_in(w_in):
    n_layers, d, _ = w_in.shape
    assert ALIGNED_DIM % REGROUP_COLS == 0 and BIG_DIM % REGROUP_COLS == 0 and FOX_HEADS <= REGROUP_HALO
    wt = jnp.transpose(w_in, (2, 0, 1))
    halo_blocks = REGROUP_COLS // REGROUP_HALO
    est = 2 * (REGROUP_COLS + REGROUP_HALO) * n_layers * d * 4 + 2 * n_layers * d * REGROUP_COLS * 2 \
        + 4 * REGROUP_COLS * n_layers * d * 4
    big = pl.pallas_call(
        _regroup_kernel,
        grid=(BIG_DIM // REGROUP_COLS,),
        in_specs=[pl.BlockSpec((REGROUP_COLS, n_layers, d), lambda c: (c, 0, 0)),
                  pl.BlockSpec((REGROUP_HALO, n_layers, d), lambda c: ((c + 1) * halo_blocks, 0, 0))],
        out_specs=pl.BlockSpec((n_layers, d, REGROUP_COLS), lambda c: (0, 0, c)),
        out_shape=jax.ShapeDtypeStruct((n_layers, d, BIG_DIM), BF16),
        compiler_params=_params(("arbitrary",), est),
        name="regroup_w_in",
    )(wt, wt)
    gate_src = jnp.concatenate([wt[ALIGNED_DIM:ALIGNED_DIM + FOX_HEADS], wt[IN_DIM - 2 * GDN_HEADS:],
                                jnp.zeros((LANES - N_GATES, n_layers, d), w_in.dtype)], axis=0)
    small = pl.pallas_call(
        _gate_cols_kernel,
        out_shape=jax.ShapeDtypeStruct((n_layers, d, LANES), BF16),
        name="regroup_gate_cols",
    )(gate_src)
    return big, small


def _rms_kernel(x_ref, g_ref, o_ref):
    x = x_ref[...]
    ms = jnp.mean(x * x, axis=-1, keepdims=True)
    o_ref[...] = (x * lax.rsqrt(ms + EPS) * g_ref[...]).astype(o_ref.dtype)


def _rmsnorm(x, gain, out_dtype, tm=512):
    m, d = x.shape
    est = 2 * tm * d * (4 + jnp.dtype(out_dtype).itemsize)
    return pl.pallas_call(
        _rms_kernel,
        grid=(m // tm,),
        in_specs=[pl.BlockSpec((tm, d), lambda i: (i, 0)),
                  pl.BlockSpec((1, d), lambda i: (0, 0))],
        out_specs=pl.BlockSpec((tm, d), lambda i: (i, 0)),
        out_shape=jax.ShapeDtypeStruct((m, d), out_dtype),
        compiler_params=_params(("arbitrary",), est),
        name="rmsnorm",
    )(x, gain.reshape(1, d))


def _mm_kernel(*refs, has_res, k_steps):
    a_ref, b_ref = refs[0], refs[1]
    res_ref = refs[2] if has_res else None
    o_ref = refs[-1]
    acc = jnp.dot(a_ref[...], b_ref[...], preferred_element_type=F32)
    if k_steps == 1:
        if has_res:
            acc = acc + res_ref[...]
        o_ref[...] = acc.astype(o_ref.dtype)
    else:
        k = pl.program_id(2)

        @pl.when(k == 0)
        def _():
            o_ref[...] = (acc + res_ref[...]) if has_res else acc

        @pl.when(k > 0)
        def _():
            o_ref[...] += acc


def _matmul(a, w, layer, out_dtype, tm, tn, residual=None, k_steps=1, name="matmul"):
    m, kdim = a.shape
    n = w.shape[2]
    tk = kdim // k_steps
    has_res = residual is not None
    assert k_steps == 1 or out_dtype == F32
    in_specs = [pl.BlockSpec((tm, tk), lambda i, j, k: (i, k)),
                pl.BlockSpec((None, tk, tn), lambda i, j, k: (layer, k, j))]
    args = [a, w]
    est = 2 * tm * tk * a.dtype.itemsize + 2 * tk * tn * w.dtype.itemsize
    if has_res:
        in_specs.append(pl.BlockSpec((tm, tn), lambda i, j, k: (i, j)))
        est += 2 * tm * tn * 4
        args.append(residual)
    est += 2 * tm * tn * jnp.dtype(out_dtype).itemsize + 2 * tm * tn * 4
    return pl.pallas_call(
        functools.partial(_mm_kernel, has_res=has_res, k_steps=k_steps),
        grid=(m // tm, n // tn, k_steps),
        in_specs=in_specs,
        out_specs=pl.BlockSpec((tm, tn), lambda i, j, k: (i, j)),
        out_shape=jax.ShapeDtypeStruct((m, n), out_dtype),
        compiler_params=_params(("arbitrary", "arbitrary", "arbitrary"), est),
        name=name,
    )(*args)


def _scan_rows(y, row, seg):
    pos = row & (seg - 1)
    s = 1
    while s < seg:
        y = y + jnp.where(pos >= s, pltpu.roll(y, s, axis=0), 0.0)
        s *= 2
    return y


def _gates_kernel(x_ref, p_ref, g_ref, gt_ref, carry_ref):
    t = pl.program_id(1)

    @pl.when(t == 0)
    def _():
        carry_ref[...] = jnp.zeros_like(carry_ref)

    tb = x_ref.shape[0]
    z = x_ref[...] + p_ref[0:1, :]
    lane = lax.broadcasted_iota(jnp.int32, z.shape, 1)
    row = lax.broadcasted_iota(jnp.int32, z.shape, 0)
    log_f = -_softplus(-z)
    beta = _sigmoid(z)
    g = -jnp.exp(p_ref[1:2, :]) * _softplus(z)
    cum_f = _scan_rows(log_f, row, tb) + carry_ref[0:1, :]
    carry_ref[0:1, :] = cum_f[tb - 1:tb, :]
    cum_g = _scan_rows(g, row, GDN_CHUNK)
    out = jnp.where(lane < LANE_BETA, cum_f, jnp.where(lane < LANE_G, beta, cum_g))
    g_ref[...] = out
    gt_ref[...] = out.T[:GATE_ROWS, :]


def _gates(small, bias_row, alog_row, batch, seq, tb=512):
    m = small.shape[0]
    nt = seq // tb
    params = jnp.zeros((SUBLANES, LANES), F32).at[0].set(bias_row).at[1].set(alog_row)
    est = 2 * tb * LANES * 4 * 3 + 16 * tb * LANES * 4
    return pl.pallas_call(
        _gates_kernel,
        grid=(batch, nt),
        in_specs=[pl.BlockSpec((tb, LANES), lambda b, t: (b * nt + t, 0)),
                  pl.BlockSpec((SUBLANES, LANES), lambda b, t: (0, 0))],
        out_specs=[pl.BlockSpec((tb, LANES), lambda b, t: (b * nt + t, 0)),
                   pl.BlockSpec((GATE_ROWS, tb), lambda b, t: (0, b * nt + t))],
        out_shape=[jax.ShapeDtypeStruct((m, LANES), F32),
                   jax.ShapeDtypeStruct((GATE_ROWS, m), F32)],
        scratch_shapes=[pltpu.VMEM((SUBLANES, LANES), F32)],
        compiler_params=_params(("arbitrary", "arbitrary"), est),
        name="gates",
    )(small, params)


def _pool_kernel(x_ref, w_ref, sc_ref, o_ref, tail_ref):
    t = pl.program_id(1)

    @pl.when(t == 0)
    def _():
        tail_ref[...] = jnp.zeros_like(tail_ref)

    tb = x_ref.shape[0]
    halo = tail_ref.shape[0]
    cg = POOL_GROUP_DIM
    x = x_ref[...].astype(F32)
    xe = jnp.concatenate([tail_ref[...], x], axis=0)
    tail_ref[...] = x[tb - halo:, :]
    pos = (t * tb + 1 + lax.broadcasted_iota(jnp.int32, (tb, cg), 0)).astype(F32)
    for gi, win in enumerate(POOL_WINDOWS):
        s = xe[:, gi * cg:(gi + 1) * cg]
        span = 1
        while span < win:
            s = s + pltpu.roll(s, span, axis=0)
            span *= 2
        mean = s[halo:, :] / jnp.minimum(pos, float(win))
        pooled = (mean - x[:, gi * cg:(gi + 1) * cg]).astype(BF16)
        y = jnp.dot(pooled, w_ref[gi], preferred_element_type=F32)
        o_ref[:, gi * cg:(gi + 1) * cg] = (y * sc_ref[:, gi * cg:(gi + 1) * cg]).astype(o_ref.dtype)


def _pool(big, pool_w, pool_scale, layer, batch, seq, tb=512):
    m = big.shape[0]
    nt = seq // tb
    halo = 16
    assert halo >= max(POOL_WINDOWS)
    est = 2 * tb * POOL_DIM * 4 + 8 * tb * POOL_DIM * 4
    return pl.pallas_call(
        _pool_kernel,
        grid=(batch, nt),
        in_specs=[pl.BlockSpec((tb, POOL_DIM), lambda b, t: (b * nt + t, 0)),
                  pl.BlockSpec((None, POOL_GROUPS, POOL_GROUP_DIM, POOL_GROUP_DIM), lambda b, t: (layer, 0, 0, 0)),
                  pl.BlockSpec((None, 1, POOL_DIM), lambda b, t: (layer, 0, 0))],
        out_specs=pl.BlockSpec((tb, POOL_DIM), lambda b, t: (b * nt + t, 0)),
        out_shape=jax.ShapeDtypeStruct((m, D_MODEL), BF16),
        scratch_shapes=[pltpu.VMEM((halo, POOL_DIM), F32)],
        compiler_params=_params(("arbitrary", "arbitrary"), est),
        name="pool_mixer",
    )(big, pool_w, pool_scale.reshape(pool_scale.shape[0], 1, POOL_DIM))


def _fox_kernel(q_ref, k_ref, v_ref, g_ref, cr_ref, mix_ref, o_ref, *, tq, heads):
    del mix_ref
    hg = pl.program_id(1)
    qi = pl.program_id(2)
    log2e = math.log2(math.e)
    scale2 = log2e / math.sqrt(HEAD_DIM)
    dh = HEAD_DIM
    gates = g_ref[...]
    lane = lax.broadcasted_iota(jnp.int32, gates.shape, 1)
    qs = [q_ref[:, g * dh:(g + 1) * dh] for g in range(heads)]
    cqs = [jnp.sum(jnp.where(lane == LANE_F + hg * heads + g, gates, 0.0), axis=1, keepdims=True) * log2e
           for g in range(heads)]
    tri = (lax.broadcasted_iota(jnp.int32, (tq, tq), 0) >= lax.broadcasted_iota(jnp.int32, (tq, tq), 1))

    def step(ki, carry, masked):
        start = pl.multiple_of(ki * tq, tq)
        kbs = [k_ref[pl.ds(start, tq), g * dh:(g + 1) * dh] for g in range(heads)]
        vbs = [v_ref[pl.ds(start, tq), g * dh:(g + 1) * dh] for g in range(heads)]
        ss = [lax.dot_general(qs[g], kbs[g], NT_DIMS, preferred_element_type=F32) for g in range(heads)]
        zps = [ss[g] * scale2 - cr_ref[g, ki] * log2e for g in range(heads)]
        if masked:
            zps = [jnp.where(tri, zp, -jnp.inf) for zp in zps]
        m_news = [jnp.maximum(carry[g][0], jnp.max(zps[g], axis=1, keepdims=True) + cqs[g]) for g in range(heads)]
        ps = [jnp.exp2(zps[g] - (m_news[g] - cqs[g])) for g in range(heads)]
        alphas = [jnp.exp2(carry[g][0] - m_news[g]) for g in range(heads)]
        l_news = [alphas[g] * carry[g][1] + jnp.sum(ps[g], axis=1, keepdims=True) for g in range(heads)]
        accs = [alphas[g] * carry[g][2] + jnp.dot(ps[g].astype(BF16), vbs[g], preferred_element_type=F32)
                for g in range(heads)]
        return tuple((m_news[g], l_news[g], accs[g]) for g in range(heads))

    init = tuple((jnp.full((tq, 1), -jnp.inf, F32), jnp.zeros((tq, 1), F32), jnp.zeros((tq, dh), F32))
                 for _ in range(heads))
    carry = lax.fori_loop(0, qi, lambda ki, c: step(ki, c, False), init)
    final = step(qi, carry, True)
    for g in range(heads):
        _, l_fin, acc = final[g]
        o_ref[:, g * dh:(g + 1) * dh] = (acc / l_fin).astype(o_ref.dtype)


def _fox_attention(big, gates, gates_t, mix, batch, seq, tq=512, heads=2):
    m = big.shape[0]
    nq = seq // tq
    width = heads * HEAD_DIM
    assert FOX_HEADS % heads == 0 and FOX_COL0 % heads == 0 and MIX_FOX_COL0 % heads == 0 and LANE_F % heads == 0
    cr = gates_t.reshape(GATE_ROWS, m // tq, 1, tq)
    est = (2 * (tq * width * 2 * 2 + 2 * seq * width * 2 + tq * LANES * 4 + heads * seq * 4 * 8)
           + heads * 10 * tq * tq * 4)
    return pl.pallas_call(
        functools.partial(_fox_kernel, tq=tq, heads=heads),
        grid=(batch, FOX_HEADS // heads, nq),
        in_specs=[
            pl.BlockSpec((tq, width), lambda b, h, i: (b * nq + i, FOX_COL0 // heads + h)),
            pl.BlockSpec((seq, width), lambda b, h, i: (b, (FOX_COL0 + FOX_HEADS) // heads + h)),
            pl.BlockSpec((seq, width), lambda b, h, i: (b, (FOX_COL0 + 2 * FOX_HEADS) // heads + h)),
            pl.BlockSpec((tq, LANES), lambda b, h, i: (b * nq + i, 0)),
            pl.BlockSpec((heads, nq, 1, tq), lambda b, h, i: (LANE_F // heads + h, b, 0, 0)),
            pl.BlockSpec(memory_space=pl.ANY),
        ],
        out_specs=pl.BlockSpec((tq, width), lambda b, h, i: (b * nq + i, MIX_FOX_COL0 // heads + h)),
        out_shape=jax.ShapeDtypeStruct(mix.shape, mix.dtype),
        input_output_aliases={5: 0},
        compiler_params=_params(("arbitrary", "arbitrary", "arbitrary"), est),
        name="fox_attention",
    )(big, big, big, gates, cr, mix)


def _bdot(a, b):
    return jnp.dot(a.astype(BF16), b.astype(BF16), preferred_element_type=F32)


def _inv_unit_lower(lows, row, col):
    n = lows[0].shape[0]
    eye = (row == col).astype(F32)
    diag = (row >> 3) == (col >> 3)
    ds = [jnp.where(diag, low, 0.0) for low in lows]
    d2s = [_bdot(d, d) for d in ds]
    d4s = [_bdot(d2, d2) for d2 in d2s]
    invs = [eye - d for d in ds]
    invs = [inv + _bdot(inv, d2) for inv, d2 in zip(invs, d2s)]
    invs = [inv + _bdot(inv, d4) for inv, d4 in zip(invs, d4s)]
    shift = 3
    while (1 << shift) < n:
        rb = row >> shift
        cb = col >> shift
        join = ((rb & 1) == 1) & (cb == rb - 1)
        inv16s = [inv.astype(BF16) for inv in invs]
        xs = [_bdot(jnp.where(join, low, 0.0), inv16) for low, inv16 in zip(lows, inv16s)]
        invs = [inv - _bdot(inv16, x) for inv, inv16, x in zip(invs, inv16s, xs)]
        shift += 1
    return invs


def _gdn_kernel(q_ref, k_ref, v_ref, z_ref, g_ref, gr_ref, cwq_ref, cwk_ref, cwv_ref, gain_ref, mix_ref,
                o_ref, state_ref, tail_ref, *, heads):
    del mix_ref
    hg = pl.program_id(1)
    t = pl.program_id(2)

    @pl.when(t == 0)
    def _():
        state_ref[...] = jnp.zeros_like(state_ref)
        tail_ref[...] = jnp.zeros_like(tail_ref)

    tb = q_ref.shape[0]
    c = GDN_CHUNK
    dk = HEAD_DIM

    def conv_silu(x_ref, w_ref, slot):
        assert GDN_CONV == 4
        x = x_ref[...].astype(F32)
        xe = jnp.concatenate([tail_ref[slot], x], axis=0)
        x1 = pltpu.roll(xe, 1, axis=0)
        w = w_ref[...]
        near = x * w[3:4, :] + x1[SUBLANES:, :] * w[2:3, :]
        far = xe * w[1:2, :] + x1 * w[0:1, :]
        y = near + pltpu.roll(far, 2, axis=0)[SUBLANES:, :]
        tail_ref[slot] = x[tb - SUBLANES:, :]
        return y * _sigmoid(y)

    def l2n(x):
        return x * lax.rsqrt(jnp.sum(x * x, axis=-1, keepdims=True) + EPS)

    q_raw = conv_silu(q_ref, cwq_ref, 0)
    k_raw = conv_silu(k_ref, cwk_ref, 1)
    v_raw = conv_silu(v_ref, cwv_ref, 2)
    gates = g_ref[...]
    lane = lax.broadcasted_iota(jnp.int32, gates.shape, 1)

    row = lax.broadcasted_iota(jnp.int32, (c, c), 0)
    col = lax.broadcasted_iota(jnp.int32, (c, c), 1)
    causal = row >= col
    strict = row > col
    last_lane = lax.broadcasted_iota(jnp.int32, (1, c), 1) == c - 1
    chunks = [slice(ci * c, (ci + 1) * c) for ci in range(tb // c)]

    hd = []
    for g in range(heads):
        cols = slice(g * dk, (g + 1) * dk)
        head = hg * heads + g
        q = l2n(q_raw[:, cols]) * (dk ** -0.5)
        k = l2n(k_raw[:, cols])
        beta = jnp.sum(jnp.where(lane == LANE_BETA + head, gates, 0.0), axis=1, keepdims=True)
        gcol = jnp.sum(jnp.where(lane == LANE_G + head, gates, 0.0), axis=1, keepdims=True)
        exp_g = jnp.exp(gcol)
        k_beta = k * beta
        hd.append(dict(cols=cols, q16=q.astype(BF16), k=k, k16=k.astype(BF16), kb16=k_beta.astype(BF16),
                       gcol=gcol, grow=gr_ref[g],
                       rhs=jnp.concatenate([v_raw[:, cols] * beta, k_beta * exp_g], axis=1).astype(BF16),
                       q_dec=q * exp_g))

    items = [(g, sl) for sl in chunks for g in range(heads)]
    decays = [jnp.exp(jnp.where(causal, hd[g]["gcol"][sl] - hd[g]["grow"][:, sl], -jnp.inf)) for g, sl in items]
    a_mats = [jnp.where(strict, lax.dot_general(hd[g]["kb16"][sl], hd[g]["k16"][sl], NT_DIMS,
                                                preferred_element_type=F32) * dec, 0.0)
              for (g, sl), dec in zip(items, decays)]
    intras = [jnp.where(causal, lax.dot_general(hd[g]["q16"][sl], hd[g]["k16"][sl], NT_DIMS,
                                                preferred_element_type=F32) * dec, 0.0).astype(BF16)
              for (g, sl), dec in zip(items, decays)]
    invs = _inv_unit_lower(a_mats, row, col)
    sols = [jnp.dot(inv.astype(BF16), hd[g]["rhs"][sl], preferred_element_type=F32)
            for (g, sl), inv in zip(items, invs)]

    states = [state_ref[g] for g in range(heads)]
    for ci, sl in enumerate(chunks):
        base = ci * heads
        g_lasts = [jnp.sum(jnp.where(last_lane, hd[g]["grow"][:, sl], 0.0), axis=1, keepdims=True)
                   for g in range(heads)]
        k_decs = [(hd[g]["k"][sl] * jnp.exp(g_lasts[g] - hd[g]["gcol"][sl])).astype(BF16) for g in range(heads)]
        wqs = [jnp.concatenate([sols[base + g][:, dk:], hd[g]["q_dec"][sl]], axis=0).astype(BF16)
               for g in range(heads)]
        wss = [jnp.dot(wqs[g], states[g].astype(BF16), preferred_element_type=F32) for g in range(heads)]
        vns = [(sols[base + g][:, :dk] - wss[g][:c]).astype(BF16) for g in range(heads)]
        outs = [wss[g][c:] + jnp.dot(intras[base + g], vns[g], preferred_element_type=F32) for g in range(heads)]
        states = [states[g] * jnp.exp(g_lasts[g])
                  + lax.dot_general(k_decs[g], vns[g], TN_DIMS, preferred_element_type=F32) for g in range(heads)]
        for g in range(heads):
            cols = hd[g]["cols"]
            z = z_ref[sl, cols].astype(F32)
            o = outs[g]
            o = o * lax.rsqrt(jnp.mean(o * o, axis=-1, keepdims=True) + EPS) * gain_ref[...]
            o_ref[sl, cols] = (o * (z * _sigmoid(z))).astype(o_ref.dtype)
    for g in range(heads):
        state_ref[g] = states[g]


def _gdn(big, gates, gates_t, conv_w, norm_gain, mix, layer, batch, seq, tb=512, heads=4):
    m = big.shape[0]
    nt = seq // tb
    width = heads * HEAD_DIM
    assert GDN_HEADS % heads == 0 and all(off % heads == 0 for off in (GDN_COL0, Z_COL0, MIX_GDN_COL0, LANE_G))
    gr = gates_t.reshape(GATE_ROWS, 1, m)
    est = 2 * (5 * tb * width * 2 + tb * LANES * 4 + heads * tb * 32) + 64 * tb * width * 4
    blk = lambda off: pl.BlockSpec((tb, width), lambda b, h, t: (b * nt + t, off // heads + h))
    cw = lambda off: pl.BlockSpec((None, GDN_CONV, width), lambda b, h, t: (layer, 0, off // heads + h))
    return pl.pallas_call(
        functools.partial(_gdn_kernel, heads=heads),
        grid=(batch, GDN_HEADS // heads, nt),
        in_specs=[
            blk(GDN_COL0), blk(GDN_COL0 + GDN_HEADS), blk(GDN_COL0 + 2 * GDN_HEADS), blk(Z_COL0),
            pl.BlockSpec((tb, LANES), lambda b, h, t: (b * nt + t, 0)),
            pl.BlockSpec((heads, 1, tb), lambda b, h, t: (LANE_G // heads + h, 0, b * nt + t)),
            cw(0), cw(GDN_HEADS), cw(2 * GDN_HEADS),
            pl.BlockSpec((None, 1, HEAD_DIM), lambda b, h, t: (layer, 0, 0)),
            pl.BlockSpec(memory_space=pl.ANY),
        ],
        out_specs=pl.BlockSpec((tb, width), lambda b, h, t: (b * nt + t, MIX_GDN_COL0 // heads + h)),
        out_shape=jax.ShapeDtypeStruct(mix.shape, mix.dtype),
        input_output_aliases={10: 0},
        scratch_shapes=[pltpu.VMEM((heads, HEAD_DIM, HEAD_DIM), F32),
                        pltpu.VMEM((3, SUBLANES, width), F32)],
        compiler_params=_params(("arbitrary", "arbitrary", "arbitrary"), est),
        name="gated_delta_rule",
    )(big, big, big, big, gates, gr, conv_w, conv_w, conv_w,
      norm_gain.reshape(norm_gain.shape[0], 1, HEAD_DIM), mix)


def _ffn_up_kernel(h_ref, wg_ref, wu_ref, cg_ref, cu_ref, o_ref, tail_ref, *, blocks_per_seq):
    i = pl.program_id(0)
    j = pl.program_id(1)
    tm = h_ref.shape[0]

    @pl.when(i % blocks_per_seq == 0)
    def _():
        tail_ref[j] = jnp.zeros(tail_ref.shape[1:], F32)

    h = h_ref[...]
    yg = jnp.dot(h, wg_ref[...], preferred_element_type=F32)
    yu = jnp.dot(h, wu_ref[...], preferred_element_type=F32)
    tail = tail_ref[j]

    def conv(y, prev, w):
        out = y * w[FFN_CONV - 1:FFN_CONV, :]
        for back in range(1, FFN_CONV):
            out = out + _shift_rows(y, prev, back) * w[FFN_CONV - 1 - back:FFN_CONV - back, :]
        return out

    ug = conv(yg, tail[:SUBLANES], cg_ref[...])
    uu = conv(yu, tail[SUBLANES:], cu_ref[...])
    tail_ref[j] = jnp.concatenate([yg[tm - SUBLANES:], yu[tm - SUBLANES:]], axis=0)
    o_ref[...] = (ug * _sigmoid(ug) * uu).astype(o_ref.dtype)


def _ffn_up(h, w_up, conv_w, layer, seq, tm=1024, tn=256):
    m, d = h.shape
    nj = FFN_DIM // tn
    est = 2 * (tm * d * 2 + 2 * d * tn * 2 + tm * tn * 2) + nj * 2 * SUBLANES * tn * 4 + 12 * tm * tn * 4
    return pl.pallas_call(
        functools.partial(_ffn_up_kernel, blocks_per_seq=seq // tm),
        grid=(m // tm, nj),
        in_specs=[pl.BlockSpec((tm, d), lambda i, j: (i, 0)),
                  pl.BlockSpec((None, d, tn), lambda i, j: (layer, 0, j)),
                  pl.BlockSpec((None, d, tn), lambda i, j: (layer, 0, nj + j)),
                  pl.BlockSpec((None, FFN_CONV, tn), lambda i, j: (layer, 0, j)),
                  pl.BlockSpec((None, FFN_CONV, tn), lambda i, j: (layer, 0, nj + j))],
        out_specs=pl.BlockSpec((tm, tn), lambda i, j: (i, j)),
        out_shape=jax.ShapeDtypeStruct((m, FFN_DIM), BF16),
        scratch_shapes=[pltpu.VMEM((nj, 2 * SUBLANES, tn), F32)],
        compiler_params=_params(("arbitrary", "arbitrary"), est),
        name="ffn_up_conv_gate",
    )(h, w_up, w_up, conv_w, conv_w)


def kernel(x, norm_mix_gain, w_in, pool_w, pool_scale, fox_f_bias, gdn_conv_w, gdn_A_log, gdn_dt_bias,
           gdn_norm_gain, w_o, norm_ffn_gain, w_up, ffn_conv_w, w_down, final_norm_gain):
    batch, seq, d = x.shape
    n_layers = norm_mix_gain.shape[0]
    assert d == D_MODEL and w_in.shape[2] == IN_DIM and seq % 1024 == 0
    w_big, w_small = _regroup_w_in(w_in)
    w_o16, w_up16, w_down16, pool_w16 = (w.astype(BF16) for w in (w_o, w_up, w_down, pool_w))
    zeros_h = jnp.zeros((n_layers, GDN_HEADS), F32)
    zeros_pad = jnp.zeros((n_layers, LANES - N_GATES), F32)
    gate_bias = jnp.concatenate([fox_f_bias.astype(F32), zeros_h, gdn_dt_bias.astype(F32), zeros_pad], axis=1)
    gate_alog = jnp.concatenate([jnp.zeros((n_layers, FOX_HEADS), F32), zeros_h, gdn_A_log.astype(F32), zeros_pad],
                                axis=1)

    xf = x.reshape(batch * seq, d).astype(F32)
    for l in range(n_layers):
        h = _rmsnorm(xf, norm_mix_gain[l], BF16)
        big = _matmul(h, w_big, l, BF16, tm=2048, tn=512, name="in_proj")
        small = _matmul(h, w_small, l, F32, tm=1024, tn=LANES, name="gate_proj")
        gates, gates_t = _gates(small, gate_bias[l], gate_alog[l], batch, seq)
        mix = _pool(big, pool_w16, pool_scale, l, batch, seq)
        mix = _fox_attention(big, gates, gates_t, mix, batch, seq)
        mix = _gdn(big, gates, gates_t, gdn_conv_w, gdn_norm_gain, mix, l, batch, seq)
        xf = _matmul(mix, w_o16, l, F32, tm=1024, tn=1024, residual=xf, name="out_proj")
        h = _rmsnorm(xf, norm_ffn_gain[l], BF16)
        act = _ffn_up(h, w_up16, ffn_conv_w, l, seq)
        xf = _matmul(act, w_down16, l, F32, tm=1024, tn=512, residual=xf, k_steps=2, name="ffn_down")
    out = _rmsnorm(xf, final_norm_gain, x.dtype)
    return out.reshape(batch, seq, d)
```

```python
import functools
import math

import jax
import jax.numpy as jnp
from jax import lax
from jax.experimental import pallas as pl
from jax.experimental.pallas import tpu as pltpu

D_MODEL = 4096
HEAD_DIM = 128
POOL_WINDOWS = (2, 4, 8, 16)
POOL_GROUPS = 4
POOL_GROUP_DIM = D_MODEL // 16
POOL_DIM = POOL_GROUPS * POOL_GROUP_DIM
ATTN_DIM = (D_MODEL - POOL_DIM) // 2
FOX_HEADS = ATTN_DIM // HEAD_DIM
GDN_DIM = D_MODEL - POOL_DIM - ATTN_DIM
GDN_HEADS = GDN_DIM // HEAD_DIM
GDN_CONV = 4
FFN_DIM = 11008
FFN_CONV = 3
EPS = 1e-6
IN_DIM = POOL_DIM + 3 * ATTN_DIM + FOX_HEADS + 3 * GDN_DIM + GDN_DIM + 2 * GDN_HEADS

LANES = 128
SUBLANES = 8
VMEM_BYTES_V7X = 64 * 1024 * 1024
VMEM_CAP = VMEM_BYTES_V7X - 8 * 1024 * 1024

ALIGNED_DIM = POOL_DIM + 3 * ATTN_DIM
BIG_DIM = ALIGNED_DIM + 3 * GDN_DIM + GDN_DIM
N_GATES = FOX_HEADS + 2 * GDN_HEADS
FOX_COL0 = POOL_DIM // LANES
GDN_COL0 = ALIGNED_DIM // LANES
Z_COL0 = (ALIGNED_DIM + 3 * GDN_DIM) // LANES
MIX_FOX_COL0 = POOL_DIM // LANES
MIX_GDN_COL0 = (POOL_DIM + ATTN_DIM) // LANES
LANE_F = 0
LANE_BETA = FOX_HEADS
LANE_G = FOX_HEADS + GDN_HEADS
GATE_ROWS = 48
GDN_CHUNK = 128
F32 = jnp.float32
BF16 = jnp.bfloat16
NT_DIMS = (((1,), (1,)), ((), ()))
TN_DIMS = (((0,), (0,)), ((), ()))


def _params(semantics, vmem_estimate):
    limit = min(int(vmem_estimate * 1.25) + (4 << 20), VMEM_CAP)
    return pltpu.CompilerParams(dimension_semantics=semantics, vmem_limit_bytes=limit)


def _sigmoid(x):
    return 1.0 / (1.0 + jnp.exp(-x))


def _softplus(x):
    return jnp.maximum(x, 0.0) + jnp.log1p(jnp.exp(-jnp.abs(x)))


def _shift_rows(x, prev, k):
    n, w = x.shape
    x3 = jnp.concatenate([prev, x], axis=0).reshape(n // SUBLANES + 1, SUBLANES, w)
    r = pltpu.roll(x3, k, axis=1)
    sub = lax.broadcasted_iota(jnp.int32, (n // SUBLANES, SUBLANES, w), 1)
    return jnp.where(sub < k, r[:-1], r[1:]).reshape(n, w)


REGROUP_COLS = 256
REGROUP_HALO = 16


def _regroup_kernel(cur_ref, nxt_ref, big_ref):
    c = pl.program_id(0)
    n_layers = cur_ref.shape[1]

    def emit(src):
        for l in range(n_layers):
            big_ref[l] = src[:, l, :].T.astype(BF16)

    @pl.when(c < ALIGNED_DIM // REGROUP_COLS)
    def _():
        emit(cur_ref[...])

    @pl.when(c >= ALIGNED_DIM // REGROUP_COLS)
    def _():
        emit(jnp.concatenate([cur_ref[FOX_HEADS:], nxt_ref[:FOX_HEADS]], axis=0))


def _gate_cols_kernel(src_ref, small_ref):
    for l in range(src_ref.shape[1]):
        small_ref[l] = src_ref[:, l, :].T.astype(BF16)


def _regroup_w_in(w_in):
    n_layers, d, _ = w_in.shape
    assert ALIGNED_DIM % REGROUP_COLS == 0 and BIG_DIM % REGROUP_COLS == 0 and FOX_HEADS <= REGROUP_HALO
    wt = jnp.transpose(w_in, (2, 0, 1))
    halo_blocks = REGROUP_COLS // REGROUP_HALO
    est = 2 * (REGROUP_COLS + REGROUP_HALO) * n_layers * d * 4 + 2 * n_layers * d * REGROUP_COLS * 2 \
        + 4 * REGROUP_COLS * n_layers * d * 4
    big = pl.pallas_call(
        _regroup_kernel,
        grid=(BIG_DIM // REGROUP_COLS,),
        in_specs=[pl.BlockSpec((REGROUP_COLS, n_layers, d), lambda c: (c, 0, 0)),
                  pl.BlockSpec((REGROUP_HALO, n_layers, d), lambda c: ((c + 1) * halo_blocks, 0, 0))],
        out_specs=pl.BlockSpec((n_layers, d, REGROUP_COLS), lambda c: (0, 0, c)),
        out_shape=jax.ShapeDtypeStruct((n_layers, d, BIG_DIM), BF16),
        compiler_params=_params(("arbitrary",), est),
        name="regroup_w_in",
    )(wt, wt)
    gate_src = jnp.concatenate([wt[ALIGNED_DIM:ALIGNED_DIM + FOX_HEADS], wt[IN_DIM - 2 * GDN_HEADS:],
                                jnp.zeros((LANES - N_GATES, n_layers, d), w_in.dtype)], axis=0)
    small = pl.pallas_call(
        _gate_cols_kernel,
        out_shape=jax.ShapeDtypeStruct((n_layers, d, LANES), BF16),
        name="regroup_gate_cols",
    )(gate_src)
    return big, small


def _rms_kernel(x_ref, g_ref, o_ref):
    x = x_ref[...]
    ms = jnp.mean(x * x, axis=-1, keepdims=True)
    o_ref[...] = (x * lax.rsqrt(ms + EPS) * g_ref[...]).astype(o_ref.dtype)


def _rmsnorm(x, gain, out_dtype, tm=512):
    m, d = x.shape
    est = 2 * tm * d * (4 + jnp.dtype(out_dtype).itemsize)
    return pl.pallas_call(
        _rms_kernel,
        grid=(m // tm,),
        in_specs=[pl.BlockSpec((tm, d), lambda i: (i, 0)),
                  pl.BlockSpec((1, d), lambda i: (0, 0))],
        out_specs=pl.BlockSpec((tm, d), lambda i: (i, 0)),
        out_shape=jax.ShapeDtypeStruct((m, d), out_dtype),
        compiler_params=_params(("arbitrary",), est),
        name="rmsnorm",
    )(x, gain.reshape(1, d))


def _mm_kernel(*refs, has_res):
    a_ref, b_ref = refs[0], refs[1]
    o_ref = refs[-1]
    acc = jnp.dot(a_ref[...], b_ref[...], preferred_element_type=F32)
    if has_res:
        acc = acc + refs[2][...]
    o_ref[...] = acc.astype(o_ref.dtype)


def _matmul(a, w, layer, out_dtype, tm, tn, residual=None, name="matmul"):
    m, kdim = a.shape
    n = w.shape[2]
    has_res = residual is not None
    in_specs = [pl.BlockSpec((tm, kdim), lambda i, j: (i, 0)),
                pl.BlockSpec((None, kdim, tn), lambda i, j: (layer, 0, j))]
    args = [a, w]
    est = 2 * tm * kdim * a.dtype.itemsize + 2 * kdim * tn * w.dtype.itemsize
    if has_res:
        in_specs.append(pl.BlockSpec((tm, tn), lambda i, j: (i, j)))
        est += 2 * tm * tn * 4
        args.append(residual)
    est += 2 * tm * tn * jnp.dtype(out_dtype).itemsize + 2 * tm * tn * 4
    return pl.pallas_call(
        functools.partial(_mm_kernel, has_res=has_res),
        grid=(m // tm, n // tn),
        in_specs=in_specs,
        out_specs=pl.BlockSpec((tm, tn), lambda i, j: (i, j)),
        out_shape=jax.ShapeDtypeStruct((m, n), out_dtype),
        compiler_params=_params(("arbitrary", "arbitrary"), est),
        name=name,
    )(*args)


def _scan_rows(y, row, seg):
    pos = row & (seg - 1)
    s = 1
    while s < seg:
        y = y + jnp.where(pos >= s, pltpu.roll(y, s, axis=0), 0.0)
        s *= 2
    return y


def _gates_kernel(x_ref, p_ref, g_ref, gt_ref, carry_ref):
    t = pl.program_id(1)

    @pl.when(t == 0)
    def _():
        carry_ref[...] = jnp.zeros_like(carry_ref)

    tb = x_ref.shape[0]
    z = x_ref[...] + p_ref[0:1, :]
    lane = lax.broadcasted_iota(jnp.int32, z.shape, 1)
    row = lax.broadcasted_iota(jnp.int32, z.shape, 0)
    log_f = -_softplus(-z)
    beta = _sigmoid(z)
    g = -jnp.exp(p_ref[1:2, :]) * _softplus(z)
    cum_f = _scan_rows(log_f, row, tb) + carry_ref[0:1, :]
    carry_ref[0:1, :] = cum_f[tb - 1:tb, :]
    cum_g = _scan_rows(g, row, GDN_CHUNK)
    out = jnp.where(lane < LANE_BETA, cum_f, jnp.where(lane < LANE_G, beta, cum_g))
    g_ref[...] = out
    gt_ref[...] = out.T[:GATE_ROWS, :]


def _gates(small, bias_row, alog_row, batch, seq, tb=512):
    m = small.shape[0]
    nt = seq // tb
    params = jnp.zeros((SUBLANES, LANES), F32).at[0].set(bias_row).at[1].set(alog_row)
    est = 2 * tb * LANES * 4 * 3 + 16 * tb * LANES * 4
    return pl.pallas_call(
        _gates_kernel,
        grid=(batch, nt),
        in_specs=[pl.BlockSpec((tb, LANES), lambda b, t: (b * nt + t, 0)),
                  pl.BlockSpec((SUBLANES, LANES), lambda b, t: (0, 0))],
        out_specs=[pl.BlockSpec((tb, LANES), lambda b, t: (b * nt + t, 0)),
                   pl.BlockSpec((GATE_ROWS, tb), lambda b, t: (0, b * nt + t))],
        out_shape=[jax.ShapeDtypeStruct((m, LANES), F32),
                   jax.ShapeDtypeStruct((GATE_ROWS, m), F32)],
        scratch_shapes=[pltpu.VMEM((SUBLANES, LANES), F32)],
        compiler_params=_params(("arbitrary", "arbitrary"), est),
        name="gates",
    )(small, params)


def _pool_kernel(x_ref, w_ref, sc_ref, o_ref, tail_ref):
    t = pl.program_id(1)

    @pl.when(t == 0)
    def _():
        tail_ref[...] = jnp.zeros_like(tail_ref)

    tb = x_ref.shape[0]
    halo = tail_ref.shape[0]
    cg = POOL_GROUP_DIM
    x = x_ref[...].astype(F32)
    xe = jnp.concatenate([tail_ref[...], x], axis=0)
    tail_ref[...] = x[tb - halo:, :]
    pos = (t * tb + 1 + lax.broadcasted_iota(jnp.int32, (tb, cg), 0)).astype(F32)
    for gi, win in enumerate(POOL_WINDOWS):
        s = xe[:, gi * cg:(gi + 1) * cg]
        span = 1
        while span < win:
            s = s + pltpu.roll(s, span, axis=0)
            span *= 2
        mean = s[halo:, :] / jnp.minimum(pos, float(win))
        pooled = (mean - x[:, gi * cg:(gi + 1) * cg]).astype(BF16)
        y = jnp.dot(pooled, w_ref[gi], preferred_element_type=F32)
        o_ref[:, gi * cg:(gi + 1) * cg] = (y * sc_ref[:, gi * cg:(gi + 1) * cg]).astype(o_ref.dtype)


def _pool(big, pool_w, pool_scale, layer, batch, seq, tb=512):
    m = big.shape[0]
    nt = seq // tb
    halo = 16
    assert halo >= max(POOL_WINDOWS)
    est = 2 * tb * POOL_DIM * 4 + 8 * tb * POOL_DIM * 4
    return pl.pallas_call(
        _pool_kernel,
        grid=(batch, nt),
        in_specs=[pl.BlockSpec((tb, POOL_DIM), lambda b, t: (b * nt + t, 0)),
                  pl.BlockSpec((None, POOL_GROUPS, POOL_GROUP_DIM, POOL_GROUP_DIM), lambda b, t: (layer, 0, 0, 0)),
                  pl.BlockSpec((None, 1, POOL_DIM), lambda b, t: (layer, 0, 0))],
        out_specs=pl.BlockSpec((tb, POOL_DIM), lambda b, t: (b * nt + t, 0)),
        out_shape=jax.ShapeDtypeStruct((m, D_MODEL), BF16),
        scratch_shapes=[pltpu.VMEM((halo, POOL_DIM), F32)],
        compiler_params=_params(("arbitrary", "arbitrary"), est),
        name="pool_mixer",
    )(big, pool_w, pool_scale.reshape(pool_scale.shape[0], 1, POOL_DIM))


def _fox_kernel(q_ref, k_ref, v_ref, g_ref, cr_ref, mix_ref, o_ref, *, tq, heads):
    del mix_ref
    hg = pl.program_id(1)
    qi = pl.program_id(2)
    log2e = math.log2(math.e)
    scale2 = log2e / math.sqrt(HEAD_DIM)
    dh = HEAD_DIM
    gates = g_ref[...]
    lane = lax.broadcasted_iota(jnp.int32, gates.shape, 1)
    qs = [q_ref[:, g * dh:(g + 1) * dh] for g in range(heads)]
    cqs = [jnp.sum(jnp.where(lane == LANE_F + hg * heads + g, gates, 0.0), axis=1, keepdims=True) * log2e
           for g in range(heads)]
    tri = (lax.broadcasted_iota(jnp.int32, (tq, tq), 0) >= lax.broadcasted_iota(jnp.int32, (tq, tq), 1))

    def step(ki, carry, masked):
        start = pl.multiple_of(ki * tq, tq)
        kbs = [k_ref[pl.ds(start, tq), g * dh:(g + 1) * dh] for g in range(heads)]
        vbs = [v_ref[pl.ds(start, tq), g * dh:(g + 1) * dh] for g in range(heads)]
        ss = [lax.dot_general(qs[g], kbs[g], NT_DIMS, preferred_element_type=F32) for g in range(heads)]
        zps = [ss[g] * scale2 - cr_ref[g, ki] * log2e for g in range(heads)]
        if masked:
            zps = [jnp.where(tri, zp, -jnp.inf) for zp in zps]
        m_news = [jnp.maximum(carry[g][0], jnp.max(zps[g], axis=1, keepdims=True) + cqs[g]) for g in range(heads)]
        ps = [jnp.exp2(zps[g] - (m_news[g] - cqs[g])) for g in range(heads)]
        alphas = [jnp.exp2(carry[g][0] - m_news[g]) for g in range(heads)]
        l_news = [alphas[g] * carry[g][1] + jnp.sum(ps[g], axis=1, keepdims=True) for g in range(heads)]
        accs = [alphas[g] * carry[g][2] + jnp.dot(ps[g].astype(BF16), vbs[g], preferred_element_type=F32)
                for g in range(heads)]
        return tuple((m_news[g], l_news[g], accs[g]) for g in range(heads))

    init = tuple((jnp.full((tq, 1), -jnp.inf, F32), jnp.zeros((tq, 1), F32), jnp.zeros((tq, dh), F32))
                 for _ in range(heads))
    carry = lax.fori_loop(0, qi, lambda ki, c: step(ki, c, False), init)
    final = step(qi, carry, True)
    for g in range(heads):
        _, l_fin, acc = final[g]
        o_ref[:, g * dh:(g + 1) * dh] = (acc / l_fin).astype(o_ref.dtype)


def _fox_attention(big, gates, gates_t, mix, batch, seq, tq=512, heads=2):
    m = big.shape[0]
    nq = seq // tq
    width = heads * HEAD_DIM
    assert FOX_HEADS % heads == 0 and FOX_COL0 % heads == 0 and MIX_FOX_COL0 % heads == 0 and LANE_F % heads == 0
    cr = gates_t.reshape(GATE_ROWS, m // tq, 1, tq)
    est = (2 * (tq * width * 2 * 2 + 2 * seq * width * 2 + tq * LANES * 4 + heads * seq * 4 * 8)
           + heads * 10 * tq * tq * 4)
    return pl.pallas_call(
        functools.partial(_fox_kernel, tq=tq, heads=heads),
        grid=(batch, FOX_HEADS // heads, nq),
        in_specs=[
            pl.BlockSpec((tq, width), lambda b, h, i: (b * nq + i, FOX_COL0 // heads + h)),
            pl.BlockSpec((seq, width), lambda b, h, i: (b, (FOX_COL0 + FOX_HEADS) // heads + h)),
            pl.BlockSpec((seq, width), lambda b, h, i: (b, (FOX_COL0 + 2 * FOX_HEADS) // heads + h)),
            pl.BlockSpec((tq, LANES), lambda b, h, i: (b * nq + i, 0)),
            pl.BlockSpec((heads, nq, 1, tq), lambda b, h, i: (LANE_F // heads + h, b, 0, 0)),
            pl.BlockSpec(memory_space=pl.ANY),
        ],
        out_specs=pl.BlockSpec((tq, width), lambda b, h, i: (b * nq + i, MIX_FOX_COL0 // heads + h)),
        out_shape=jax.ShapeDtypeStruct(mix.shape, mix.dtype),
        input_output_aliases={5: 0},
        compiler_params=_params(("arbitrary", "arbitrary", "arbitrary"), est),
        name="fox_attention",
    )(big, big, big, gates, cr, mix)


def _bdot(a, b):
    return jnp.dot(a.astype(BF16), b.astype(BF16), preferred_element_type=F32)


def _inv_unit_lower(lows, row, col):
    n = lows[0].shape[0]
    eye = (row == col).astype(F32)
    diag = (row >> 3) == (col >> 3)
    ds = [jnp.where(diag, low, 0.0) for low in lows]
    d2s = [_bdot(d, d) for d in ds]
    d4s = [_bdot(d2, d2) for d2 in d2s]
    invs = [eye - d for d in ds]
    invs = [inv + _bdot(inv, d2) for inv, d2 in zip(invs, d2s)]
    invs = [inv + _bdot(inv, d4) for inv, d4 in zip(invs, d4s)]
    shift = 3
    while (1 << shift) < n:
        rb = row >> shift
        cb = col >> shift
        join = ((rb & 1) == 1) & (cb == rb - 1)
        inv16s = [inv.astype(BF16) for inv in invs]
        xs = [_bdot(jnp.where(join, low, 0.0), inv16) for low, inv16 in zip(lows, inv16s)]
        invs = [inv - _bdot(inv16, x) for inv, inv16, x in zip(invs, inv16s, xs)]
        shift += 1
    return invs


def _gdn_kernel(q_ref, k_ref, v_ref, z_ref, g_ref, gr_ref, cwq_ref, cwk_ref, cwv_ref, gain_ref, mix_ref,
                o_ref, state_ref, tail_ref, *, heads):
    del mix_ref
    hg = pl.program_id(1)
    t = pl.program_id(2)

    @pl.when(t == 0)
    def _():
        state_ref[...] = jnp.zeros_like(state_ref)
        tail_ref[...] = jnp.zeros_like(tail_ref)

    tb = q_ref.shape[0]
    c = GDN_CHUNK
    dk = HEAD_DIM

    def conv_silu(x_ref, w_ref, slot):
        assert GDN_CONV == 4
        x = x_ref[...].astype(F32)
        xe = jnp.concatenate([tail_ref[slot], x], axis=0)
        x1 = pltpu.roll(xe, 1, axis=0)
        w = w_ref[...]
        near = x * w[3:4, :] + x1[SUBLANES:, :] * w[2:3, :]
        far = xe * w[1:2, :] + x1 * w[0:1, :]
        y = near + pltpu.roll(far, 2, axis=0)[SUBLANES:, :]
        tail_ref[slot] = x[tb - SUBLANES:, :]
        return y * _sigmoid(y)

    def l2n(x):
        return x * lax.rsqrt(jnp.sum(x * x, axis=-1, keepdims=True) + EPS)

    q_raw = conv_silu(q_ref, cwq_ref, 0)
    k_raw = conv_silu(k_ref, cwk_ref, 1)
    v_raw = conv_silu(v_ref, cwv_ref, 2)
    gates = g_ref[...]
    lane = lax.broadcasted_iota(jnp.int32, gates.shape, 1)

    row = lax.broadcasted_iota(jnp.int32, (c, c), 0)
    col = lax.broadcasted_iota(jnp.int32, (c, c), 1)
    causal = row >= col
    strict = row > col
    last_lane = lax.broadcasted_iota(jnp.int32, (1, c), 1) == c - 1
    chunks = [slice(ci * c, (ci + 1) * c) for ci in range(tb // c)]

    hd = []
    for g in range(heads):
        cols = slice(g * dk, (g + 1) * dk)
        head = hg * heads + g
        q = l2n(q_raw[:, cols]) * (dk ** -0.5)
        k = l2n(k_raw[:, cols])
        beta = jnp.sum(jnp.where(lane == LANE_BETA + head, gates, 0.0), axis=1, keepdims=True)
        gcol = jnp.sum(jnp.where(lane == LANE_G + head, gates, 0.0), axis=1, keepdims=True)
        exp_g = jnp.exp(gcol)
        k_beta = k * beta
        hd.append(dict(cols=cols, q16=q.astype(BF16), k=k, k16=k.astype(BF16), kb16=k_beta.astype(BF16),
                       gcol=gcol, grow=gr_ref[g],
                       rhs=jnp.concatenate([v_raw[:, cols] * beta, k_beta * exp_g], axis=1).astype(BF16),
                       q_dec=q * exp_g))

    items = [(g, sl) for sl in chunks for g in range(heads)]
    decays = [jnp.exp(jnp.where(causal, hd[g]["gcol"][sl] - hd[g]["grow"][:, sl], -jnp.inf)) for g, sl in items]
    a_mats = [jnp.where(strict, lax.dot_general(hd[g]["kb16"][sl], hd[g]["k16"][sl], NT_DIMS,
                                                preferred_element_type=F32) * dec, 0.0)
              for (g, sl), dec in zip(items, decays)]
    intras = [jnp.where(causal, lax.dot_general(hd[g]["q16"][sl], hd[g]["k16"][sl], NT_DIMS,
                                                preferred_element_type=F32) * dec, 0.0).astype(BF16)
              for (g, sl), dec in zip(items, decays)]
    invs = _inv_unit_lower(a_mats, row, col)
    sols = [jnp.dot(inv.astype(BF16), hd[g]["rhs"][sl], preferred_element_type=F32)
            for (g, sl), inv in zip(items, invs)]

    states = [state_ref[g] for g in range(heads)]
    for ci, sl in enumerate(chunks):
        base = ci * heads
        g_lasts = [jnp.sum(jnp.where(last_lane, hd[g]["grow"][:, sl], 0.0), axis=1, keepdims=True)
                   for g in range(heads)]
        k_decs = [(hd[g]["k"][sl] * jnp.exp(g_lasts[g] - hd[g]["gcol"][sl])).astype(BF16) for g in range(heads)]
        wqs = [jnp.concatenate([sols[base + g][:, dk:], hd[g]["q_dec"][sl]], axis=0).astype(BF16)
               for g in range(heads)]
        wss = [jnp.dot(wqs[g], states[g].astype(BF16), preferred_element_type=F32) for g in range(heads)]
        vns = [(sols[base + g][:, :dk] - wss[g][:c]).astype(BF16) for g in range(heads)]
        outs = [wss[g][c:] + jnp.dot(intras[base + g], vns[g], preferred_element_type=F32) for g in range(heads)]
        states = [states[g] * jnp.exp(g_lasts[g])
                  + lax.dot_general(k_decs[g], vns[g], TN_DIMS, preferred_element_type=F32) for g in range(heads)]
        for g in range(heads):
            cols = hd[g]["cols"]
            z = z_ref[sl, cols].astype(F32)
            o = outs[g]
            o = o * lax.rsqrt(jnp.mean(o * o, axis=-1, keepdims=True) + EPS) * gain_ref[...]
            o_ref[sl, cols] = (o * (z * _sigmoid(z))).astype(o_ref.dtype)
    for g in range(heads):
        state_ref[g] = states[g]


def _gdn(big, gates, gates_t, conv_w, norm_gain, mix, layer, batch, seq, tb=512, heads=4):
    m = big.shape[0]
    nt = seq // tb
    width = heads * HEAD_DIM
    assert GDN_HEADS % heads == 0 and all(off % heads == 0 for off in (GDN_COL0, Z_COL0, MIX_GDN_COL0, LANE_G))
    gr = gates_t.reshape(GATE_ROWS, 1, m)
    est = 2 * (5 * tb * width * 2 + tb * LANES * 4 + heads * tb * 32) + 64 * tb * width * 4
    blk = lambda off: pl.BlockSpec((tb, width), lambda b, h, t: (b * nt + t, off // heads + h))
    cw = lambda off: pl.BlockSpec((None, GDN_CONV, width), lambda b, h, t: (layer, 0, off // heads + h))
    return pl.pallas_call(
        functools.partial(_gdn_kernel, heads=heads),
        grid=(batch, GDN_HEADS // heads, nt),
        in_specs=[
            blk(GDN_COL0), blk(GDN_COL0 + GDN_HEADS), blk(GDN_COL0 + 2 * GDN_HEADS), blk(Z_COL0),
            pl.BlockSpec((tb, LANES), lambda b, h, t: (b * nt + t, 0)),
            pl.BlockSpec((heads, 1, tb), lambda b, h, t: (LANE_G // heads + h, 0, b * nt + t)),
            cw(0), cw(GDN_HEADS), cw(2 * GDN_HEADS),
            pl.BlockSpec((None, 1, HEAD_DIM), lambda b, h, t: (layer, 0, 0)),
            pl.BlockSpec(memory_space=pl.ANY),
        ],
        out_specs=pl.BlockSpec((tb, width), lambda b, h, t: (b * nt + t, MIX_GDN_COL0 // heads + h)),
        out_shape=jax.ShapeDtypeStruct(mix.shape, mix.dtype),
        input_output_aliases={10: 0},
        scratch_shapes=[pltpu.VMEM((heads, HEAD_DIM, HEAD_DIM), F32),
                        pltpu.VMEM((3, SUBLANES, width), F32)],
        compiler_params=_params(("arbitrary", "arbitrary", "arbitrary"), est),
        name="gated_delta_rule",
    )(big, big, big, big, gates, gr, conv_w, conv_w, conv_w,
      norm_gain.reshape(norm_gain.shape[0], 1, HEAD_DIM), mix)


def _ffn_up_kernel(h_ref, wg_ref, wu_ref, cg_ref, cu_ref, o_ref, tail_ref, *, blocks_per_seq):
    i = pl.program_id(0)
    j = pl.program_id(1)
    tm = h_ref.shape[0]

    @pl.when(i % blocks_per_seq == 0)
    def _():
        tail_ref[j] = jnp.zeros(tail_ref.shape[1:], F32)

    h = h_ref[...]
    yg = jnp.dot(h, wg_ref[...], preferred_element_type=F32)
    yu = jnp.dot(h, wu_ref[...], preferred_element_type=F32)
    tail = tail_ref[j]

    def conv(y, prev, w):
        out = y * w[FFN_CONV - 1:FFN_CONV, :]
        for back in range(1, FFN_CONV):
            out = out + _shift_rows(y, prev, back) * w[FFN_CONV - 1 - back:FFN_CONV - back, :]
        return out

    ug = conv(yg, tail[:SUBLANES], cg_ref[...])
    uu = conv(yu, tail[SUBLANES:], cu_ref[...])
    tail_ref[j] = jnp.concatenate([yg[tm - SUBLANES:], yu[tm - SUBLANES:]], axis=0)
    o_ref[...] = (ug * _sigmoid(ug) * uu).astype(o_ref.dtype)


def _ffn_up(h, w_up, conv_w, layer, seq, tm=1024, tn=256):
    m, d = h.shape
    nj = FFN_DIM // tn
    est = 2 * (tm * d * 2 + 2 * d * tn * 2 + tm * tn * 2) + nj * 2 * SUBLANES * tn * 4 + 12 * tm * tn * 4
    return pl.pallas_call(
        functools.partial(_ffn_up_kernel, blocks_per_seq=seq // tm),
        grid=(m // tm, nj),
        in_specs=[pl.BlockSpec((tm, d), lambda i, j: (i, 0)),
                  pl.BlockSpec((None, d, tn), lambda i, j: (layer, 0, j)),
                  pl.BlockSpec((None, d, tn), lambda i, j: (layer, 0, nj + j)),
                  pl.BlockSpec((None, FFN_CONV, tn), lambda i, j: (layer, 0, j)),
                  pl.BlockSpec((None, FFN_CONV, tn), lambda i, j: (layer, 0, nj + j))],
        out_specs=pl.BlockSpec((tm, tn), lambda i, j: (i, j)),
        out_shape=jax.ShapeDtypeStruct((m, FFN_DIM), BF16),
        scratch_shapes=[pltpu.VMEM((nj, 2 * SUBLANES, tn), F32)],
        compiler_params=_params(("arbitrary", "arbitrary"), est),
        name="ffn_up_conv_gate",
    )(h, w_up, w_up, conv_w, conv_w)


def kernel(x, norm_mix_gain, w_in, pool_w, pool_scale, fox_f_bias, gdn_conv_w, gdn_A_log, gdn_dt_bias,
           gdn_norm_gain, w_o, norm_ffn_gain, w_up, ffn_conv_w, w_down, final_norm_gain):
    batch, seq, d = x.shape
    n_layers = norm_mix_gain.shape[0]
    assert d == D_MODEL and w_in.shape[2] == IN_DIM and seq % 1024 == 0
    w_big, w_small = _regroup_w_in(w_in)
    w_o16, w_up16, w_down16, pool_w16 = (w.astype(BF16) for w in (w_o, w_up, w_down, pool_w))
    zeros_h = jnp.zeros((n_layers, GDN_HEADS), F32)
    zeros_pad = jnp.zeros((n_layers, LANES - N_GATES), F32)
    gate_bias = jnp.concatenate([fox_f_bias.astype(F32), zeros_h, gdn_dt_bias.astype(F32), zeros_pad], axis=1)
    gate_alog = jnp.concatenate([jnp.zeros((n_layers, FOX_HEADS), F32), zeros_h, gdn_A_log.astype(F32), zeros_pad],
                                axis=1)

    xf = x.reshape(batch * seq, d).astype(F32)
    for l in range(n_layers):
        h = _rmsnorm(xf, norm_mix_gain[l], BF16)
        big = _matmul(h, w_big, l, BF16, tm=2048, tn=512, name="in_proj")
        small = _matmul(h, w_small, l, F32, tm=1024, tn=LANES, name="gate_proj")
        gates, gates_t = _gates(small, gate_bias[l], gate_alog[l], batch, seq)
        mix = _pool(big, pool_w16, pool_scale, l, batch, seq)
        mix = _fox_attention(big, gates, gates_t, mix, batch, seq)
        mix = _gdn(big, gates, gates_t, gdn_conv_w, gdn_norm_gain, mix, l, batch, seq)
        xf = _matmul(mix, w_o16, l, F32, tm=1024, tn=1024, residual=xf, name="out_proj")
        h = _rmsnorm(xf, norm_ffn_gain[l], BF16)
        act = _ffn_up(h, w_up16, ffn_conv_w, l, seq)
        xf = _matmul(act, w_down16, l, F32, tm=512, tn=512, residual=xf, name="ffn_down")
    out = _rmsnorm(xf, final_norm_gain, x.dtype)
    return out.reshape(batch, seq, d)
```

```python
import functools
import math

import jax
import jax.numpy as jnp
from jax import lax
from jax.experimental import pallas as pl
from jax.experimental.pallas import tpu as pltpu

D_MODEL = 4096
HEAD_DIM = 128
POOL_WINDOWS = (2, 4, 8, 16)
POOL_GROUPS = 4
POOL_GROUP_DIM = D_MODEL // 16
POOL_DIM = POOL_GROUPS * POOL_GROUP_DIM
ATTN_DIM = (D_MODEL - POOL_DIM) // 2
FOX_HEADS = ATTN_DIM // HEAD_DIM
GDN_DIM = D_MODEL - POOL_DIM - ATTN_DIM
GDN_HEADS = GDN_DIM // HEAD_DIM
GDN_CONV = 4
FFN_DIM = 11008
FFN_CONV = 3
EPS = 1e-6
IN_DIM = POOL_DIM + 3 * ATTN_DIM + FOX_HEADS + 3 * GDN_DIM + GDN_DIM + 2 * GDN_HEADS

LANES = 128
SUBLANES = 8
VMEM_BYTES_V7X = 64 * 1024 * 1024
VMEM_CAP = VMEM_BYTES_V7X - 8 * 1024 * 1024

ALIGNED_DIM = POOL_DIM + 3 * ATTN_DIM
BIG_DIM = ALIGNED_DIM + 3 * GDN_DIM + GDN_DIM
N_GATES = FOX_HEADS + 2 * GDN_HEADS
FOX_COL0 = POOL_DIM // LANES
GDN_COL0 = ALIGNED_DIM // LANES
Z_COL0 = (ALIGNED_DIM + 3 * GDN_DIM) // LANES
MIX_FOX_COL0 = POOL_DIM // LANES
MIX_GDN_COL0 = (POOL_DIM + ATTN_DIM) // LANES
LANE_F = 0
LANE_BETA = FOX_HEADS
LANE_G = FOX_HEADS + GDN_HEADS
GATE_ROWS = 48
GDN_CHUNK = 128

F32 = jnp.float32
BF16 = jnp.bfloat16
NT_DIMS = (((1,), (1,)), ((), ()))
TN_DIMS = (((0,), (0,)), ((), ()))


def _params(semantics, vmem_estimate):
    limit = min(int(vmem_estimate * 1.25) + (4 << 20), VMEM_CAP)
    return pltpu.CompilerParams(dimension_semantics=semantics, vmem_limit_bytes=limit)


def _sigmoid(x):
    return 1.0 / (1.0 + jnp.exp(-x))


def _softplus(x):
    return jnp.maximum(x, 0.0) + jnp.log1p(jnp.exp(-jnp.abs(x)))


def _shift_rows(x, prev, k):
    n, w = x.shape
    x3 = jnp.concatenate([prev, x], axis=0).reshape(n // SUBLANES + 1, SUBLANES, w)
    r = pltpu.roll(x3, k, axis=1)
    sub = lax.broadcasted_iota(jnp.int32, (n // SUBLANES, SUBLANES, w), 1)
    return jnp.where(sub < k, r[:-1], r[1:]).reshape(n, w)


REGROUP_COLS = 256
REGROUP_HALO = 16


def _regroup_kernel(cur_ref, nxt_ref, big_ref):
    c = pl.program_id(0)
    n_layers = cur_ref.shape[1]

    def emit(src):
        for l in range(n_layers):
            big_ref[l] = src[:, l, :].T.astype(BF16)

    @pl.when(c < ALIGNED_DIM // REGROUP_COLS)
    def _():
        emit(cur_ref[...])

    @pl.when(c >= ALIGNED_DIM // REGROUP_COLS)
    def _():
        emit(jnp.concatenate([cur_ref[FOX_HEADS:], nxt_ref[:FOX_HEADS]], axis=0))


def _gate_cols_kernel(src_ref, small_ref):
    for l in range(src_ref.shape[1]):
        small_ref[l] = src_ref[:, l, :].T.astype(BF16)


def _regroup_w_in(w_in):
    n_layers, d, _ = w_in.shape
    assert ALIGNED_DIM % REGROUP_COLS == 0 and BIG_DIM % REGROUP_COLS == 0 and FOX_HEADS <= REGROUP_HALO
    wt = jnp.transpose(w_in, (2, 0, 1))
    halo_blocks = REGROUP_COLS // REGROUP_HALO
    est = 2 * (REGROUP_COLS + REGROUP_HALO) * n_layers * d * 4 + 2 * n_layers * d * REGROUP_COLS * 2 \
        + 4 * REGROUP_COLS * n_layers * d * 4
    big = pl.pallas_call(
        _regroup_kernel,
        grid=(BIG_DIM // REGROUP_COLS,),
        in_specs=[pl.BlockSpec((REGROUP_COLS, n_layers, d), lambda c: (c, 0, 0)),
                  pl.BlockSpec((REGROUP_HALO, n_layers, d), lambda c: ((c + 1) * halo_blocks, 0, 0))],
        out_specs=pl.BlockSpec((n_layers, d, REGROUP_COLS), lambda c: (0, 0, c)),
        out_shape=jax.ShapeDtypeStruct((n_layers, d, BIG_DIM), BF16),
        compiler_params=_params(("arbitrary",), est),
        name="regroup_w_in",
    )(wt, wt)
    gate_src = jnp.concatenate([wt[ALIGNED_DIM:ALIGNED_DIM + FOX_HEADS], wt[IN_DIM - 2 * GDN_HEADS:],
                                jnp.zeros((LANES - N_GATES, n_layers, d), w_in.dtype)], axis=0)
    small = pl.pallas_call(
        _gate_cols_kernel,
        out_shape=jax.ShapeDtypeStruct((n_layers, d, LANES), BF16),
        name="regroup_gate_cols",
    )(gate_src)
    return big, small


def _rms_kernel(x_ref, g_ref, o_ref):
    x = x_ref[...]
    ms = jnp.mean(x * x, axis=-1, keepdims=True)
    o_ref[...] = (x * lax.rsqrt(ms + EPS) * g_ref[...]).astype(o_ref.dtype)


def _rmsnorm(x, gain, out_dtype, tm=512):
    m, d = x.shape
    est = 2 * tm * d * (4 + jnp.dtype(out_dtype).itemsize)
    return pl.pallas_call(
        _rms_kernel,
        grid=(m // tm,),
        in_specs=[pl.BlockSpec((tm, d), lambda i: (i, 0)),
                  pl.BlockSpec((1, d), lambda i: (0, 0))],
        out_specs=pl.BlockSpec((tm, d), lambda i: (i, 0)),
        out_shape=jax.ShapeDtypeStruct((m, d), out_dtype),
        compiler_params=_params(("arbitrary",), est),
        name="rmsnorm",
    )(x, gain.reshape(1, d))


BF16_TILE_ROWS = 16


def _cast_specs(srcs, layer, steps, step_of):
    in_specs, out_specs, out_shapes, vmem = [], [], [], 0
    for src in srcs:
        _, rows, cols = src.shape
        rb = next(r for r in range(BF16_TILE_ROWS, rows + 1, BF16_TILE_ROWS) if rows % r == 0 and rows // r <= steps)
        blk = lambda *idx, n_blocks=rows // rb: jnp.minimum(step_of(*idx), n_blocks - 1)
        in_specs.append(pl.BlockSpec((None, rb, cols), lambda *idx, blk=blk: (layer, blk(*idx), 0)))
        out_specs.append(pl.BlockSpec((None, rb, cols), lambda *idx, blk=blk: (0, blk(*idx), 0)))
        out_shapes.append(jax.ShapeDtypeStruct((1, rows, cols), BF16))
        vmem += 2 * rb * cols * (4 + 2)
    return in_specs, out_specs, out_shapes, vmem


def _mm_kernel(*refs, has_res, n_casts):
    a_ref, b_ref = refs[0], refs[1]
    n_in = 2 + int(has_res) + n_casts
    o_ref = refs[n_in]
    acc = jnp.dot(a_ref[...], b_ref[...], preferred_element_type=F32)
    if has_res:
        acc = acc + refs[2][...]
    o_ref[...] = acc.astype(o_ref.dtype)
    for c in range(n_casts):
        refs[n_in + 1 + c][...] = refs[n_in - n_casts + c][...].astype(BF16)


def _matmul(a, w, layer, out_dtype, tm, tn, residual=None, casts=(), name="matmul"):
    m, kdim = a.shape
    n = w.shape[2]
    has_res = residual is not None
    nj = n // tn
    in_specs = [pl.BlockSpec((tm, kdim), lambda i, j: (i, 0)),
                pl.BlockSpec((None, kdim, tn), lambda i, j: (layer, 0, j))]
    args = [a, w]
    est = 2 * tm * kdim * a.dtype.itemsize + 2 * kdim * tn * w.dtype.itemsize
    if has_res:
        in_specs.append(pl.BlockSpec((tm, tn), lambda i, j: (i, j)))
        est += 2 * tm * tn * 4
        args.append(residual)
    est += 2 * tm * tn * jnp.dtype(out_dtype).itemsize + 2 * tm * tn * 4
    c_in, c_out, c_shapes, c_vmem = _cast_specs(casts, layer, (m // tm) * nj, lambda i, j: i * nj + j)
    out = pl.pallas_call(
        functools.partial(_mm_kernel, has_res=has_res, n_casts=len(casts)),
        grid=(m // tm, nj),
        in_specs=in_specs + c_in,
        out_specs=[pl.BlockSpec((tm, tn), lambda i, j: (i, j))] + c_out,
        out_shape=[jax.ShapeDtypeStruct((m, n), out_dtype)] + c_shapes,
        compiler_params=_params(("arbitrary", "arbitrary"), est + c_vmem),
        name=name,
    )(*args, *casts)
    return out[0] if not casts else tuple(out)


def _scan_rows(y, row, seg):
    pos = row & (seg - 1)
    s = 1
    while s < seg:
        y = y + jnp.where(pos >= s, pltpu.roll(y, s, axis=0), 0.0)
        s *= 2
    return y


def _gates_kernel(x_ref, p_ref, g_ref, gt_ref, carry_ref):
    t = pl.program_id(1)

    @pl.when(t == 0)
    def _():
        carry_ref[...] = jnp.zeros_like(carry_ref)

    tb = x_ref.shape[0]
    z = x_ref[...] + p_ref[0:1, :]
    lane = lax.broadcasted_iota(jnp.int32, z.shape, 1)
    row = lax.broadcasted_iota(jnp.int32, z.shape, 0)
    log_f = -_softplus(-z)
    beta = _sigmoid(z)
    g = -jnp.exp(p_ref[1:2, :]) * _softplus(z)
    cum_f = _scan_rows(log_f, row, tb) + carry_ref[0:1, :]
    carry_ref[0:1, :] = cum_f[tb - 1:tb, :]
    cum_g = _scan_rows(g, row, GDN_CHUNK)
    out = jnp.where(lane < LANE_BETA, cum_f, jnp.where(lane < LANE_G, beta, cum_g))
    g_ref[...] = out
    gt_ref[...] = out.T[:GATE_ROWS, :]


def _gates(small, bias_row, alog_row, batch, seq, tb=512):
    m = small.shape[0]
    nt = seq // tb
    params = jnp.zeros((SUBLANES, LANES), F32).at[0].set(bias_row).at[1].set(alog_row)
    est = 2 * tb * LANES * 4 * 3 + 16 * tb * LANES * 4
    return pl.pallas_call(
        _gates_kernel,
        grid=(batch, nt),
        in_specs=[pl.BlockSpec((tb, LANES), lambda b, t: (b * nt + t, 0)),
                  pl.BlockSpec((SUBLANES, LANES), lambda b, t: (0, 0))],
        out_specs=[pl.BlockSpec((tb, LANES), lambda b, t: (b * nt + t, 0)),
                   pl.BlockSpec((GATE_ROWS, tb), lambda b, t: (0, b * nt + t))],
        out_shape=[jax.ShapeDtypeStruct((m, LANES), F32),
                   jax.ShapeDtypeStruct((GATE_ROWS, m), F32)],
        scratch_shapes=[pltpu.VMEM((SUBLANES, LANES), F32)],
        compiler_params=_params(("arbitrary", "arbitrary"), est),
        name="gates",
    )(small, params)


def _pool_kernel(x_ref, w_ref, sc_ref, o_ref, tail_ref):
    t = pl.program_id(1)

    @pl.when(t == 0)
    def _():
        tail_ref[...] = jnp.zeros_like(tail_ref)

    tb = x_ref.shape[0]
    halo = tail_ref.shape[0]
    cg = POOL_GROUP_DIM
    x = x_ref[...].astype(F32)
    xe = jnp.concatenate([tail_ref[...], x], axis=0)
    tail_ref[...] = x[tb - halo:, :]
    pos = (t * tb + 1 + lax.broadcasted_iota(jnp.int32, (tb, cg), 0)).astype(F32)
    for gi, win in enumerate(POOL_WINDOWS):
        s = xe[:, gi * cg:(gi + 1) * cg]
        span = 1
        while span < win:
            s = s + pltpu.roll(s, span, axis=0)
            span *= 2
        mean = s[halo:, :] / jnp.minimum(pos, float(win))
        pooled = (mean - x[:, gi * cg:(gi + 1) * cg]).astype(BF16)
        y = jnp.dot(pooled, w_ref[gi], preferred_element_type=F32)
        o_ref[:, gi * cg:(gi + 1) * cg] = (y * sc_ref[:, gi * cg:(gi + 1) * cg]).astype(o_ref.dtype)


def _pool(big, pool_w, pool_scale, layer, batch, seq, tb=512):
    m = big.shape[0]
    nt = seq // tb
    halo = 16
    assert halo >= max(POOL_WINDOWS)
    est = 2 * tb * POOL_DIM * 4 + 8 * tb * POOL_DIM * 4
    return pl.pallas_call(
        _pool_kernel,
        grid=(batch, nt),
        in_specs=[pl.BlockSpec((tb, POOL_DIM), lambda b, t: (b * nt + t, 0)),
                  pl.BlockSpec((None, POOL_GROUPS, POOL_GROUP_DIM, POOL_GROUP_DIM), lambda b, t: (layer, 0, 0, 0)),
                  pl.BlockSpec((None, 1, POOL_DIM), lambda b, t: (layer, 0, 0))],
        out_specs=pl.BlockSpec((tb, POOL_DIM), lambda b, t: (b * nt + t, 0)),
        out_shape=jax.ShapeDtypeStruct((m, D_MODEL), BF16),
        scratch_shapes=[pltpu.VMEM((halo, POOL_DIM), F32)],
        compiler_params=_params(("arbitrary", "arbitrary"), est),
        name="pool_mixer",
    )(big, pool_w, pool_scale.reshape(pool_scale.shape[0], 1, POOL_DIM))


def _fox_kernel(q_ref, k_ref, v_ref, g_ref, cr_ref, mix_ref, o_ref, *, tq, heads):
    del mix_ref
    hg = pl.program_id(1)
    qi = pl.program_id(2)
    log2e = math.log2(math.e)
    scale2 = log2e / math.sqrt(HEAD_DIM)
    dh = HEAD_DIM
    gates = g_ref[...]
    lane = lax.broadcasted_iota(jnp.int32, gates.shape, 1)
    qs = [q_ref[:, g * dh:(g + 1) * dh] for g in range(heads)]
    cqs = [jnp.sum(jnp.where(lane == LANE_F + hg * heads + g, gates, 0.0), axis=1, keepdims=True) * log2e
           for g in range(heads)]
    tri = (lax.broadcasted_iota(jnp.int32, (tq, tq), 0) >= lax.broadcasted_iota(jnp.int32, (tq, tq), 1))

    def step(ki, carry, masked):
        start = pl.multiple_of(ki * tq, tq)
        kbs = [k_ref[pl.ds(start, tq), g * dh:(g + 1) * dh] for g in range(heads)]
        vbs = [v_ref[pl.ds(start, tq), g * dh:(g + 1) * dh] for g in range(heads)]
        ss = [lax.dot_general(qs[g], kbs[g], NT_DIMS, preferred_element_type=F32) for g in range(heads)]
        zps = [ss[g] * scale2 - cr_ref[g, ki] * log2e for g in range(heads)]
        if masked:
            zps = [jnp.where(tri, zp, -jnp.inf) for zp in zps]
        m_news = [jnp.maximum(carry[g][0], jnp.max(zps[g], axis=1, keepdims=True) + cqs[g]) for g in range(heads)]
        ps = [jnp.exp2(zps[g] - (m_news[g] - cqs[g])) for g in range(heads)]
        alphas = [jnp.exp2(carry[g][0] - m_news[g]) for g in range(heads)]
        l_news = [alphas[g] * carry[g][1] + jnp.sum(ps[g], axis=1, keepdims=True) for g in range(heads)]
        accs = [alphas[g] * carry[g][2] + jnp.dot(ps[g].astype(BF16), vbs[g], preferred_element_type=F32)
                for g in range(heads)]
        return tuple((m_news[g], l_news[g], accs[g]) for g in range(heads))

    init = tuple((jnp.full((tq, 1), -jnp.inf, F32), jnp.zeros((tq, 1), F32), jnp.zeros((tq, dh), F32))
                 for _ in range(heads))
    carry = lax.fori_loop(0, qi, lambda ki, c: step(ki, c, False), init)
    final = step(qi, carry, True)
    for g in range(heads):
        _, l_fin, acc = final[g]
        o_ref[:, g * dh:(g + 1) * dh] = (acc / l_fin).astype(o_ref.dtype)


def _fox_attention(big, gates, gates_t, mix, batch, seq, tq=512, heads=2):
    m = big.shape[0]
    nq = seq // tq
    width = heads * HEAD_DIM
    assert FOX_HEADS % heads == 0 and FOX_COL0 % heads == 0 and MIX_FOX_COL0 % heads == 0 and LANE_F % heads == 0
    cr = gates_t.reshape(GATE_ROWS, m // tq, 1, tq)
    est = (2 * (tq * width * 2 * 2 + 2 * seq * width * 2 + tq * LANES * 4 + heads * seq * 4 * 8)
           + heads * 10 * tq * tq * 4)
    return pl.pallas_call(
        functools.partial(_fox_kernel, tq=tq, heads=heads),
        grid=(batch, FOX_HEADS // heads, nq),
        in_specs=[
            pl.BlockSpec((tq, width), lambda b, h, i: (b * nq + i, FOX_COL0 // heads + h)),
            pl.BlockSpec((seq, width), lambda b, h, i: (b, (FOX_COL0 + FOX_HEADS) // heads + h)),
            pl.BlockSpec((seq, width), lambda b, h, i: (b, (FOX_COL0 + 2 * FOX_HEADS) // heads + h)),
            pl.BlockSpec((tq, LANES), lambda b, h, i: (b * nq + i, 0)),
            pl.BlockSpec((heads, nq, 1, tq), lambda b, h, i: (LANE_F // heads + h, b, 0, 0)),
            pl.BlockSpec(memory_space=pl.ANY),
        ],
        out_specs=pl.BlockSpec((tq, width), lambda b, h, i: (b * nq + i, MIX_FOX_COL0 // heads + h)),
        out_shape=jax.ShapeDtypeStruct(mix.shape, mix.dtype),
        input_output_aliases={5: 0},
        compiler_params=_params(("arbitrary", "arbitrary", "arbitrary"), est),
        name="fox_attention",
    )(big, big, big, gates, cr, mix)


def _bdot(a, b):
    return jnp.dot(a.astype(BF16), b.astype(BF16), preferred_element_type=F32)


def _inv_unit_lower(lows, row, col):
    n = lows[0].shape[0]
    eye = (row == col).astype(F32)
    diag = (row >> 3) == (col >> 3)
    ds = [jnp.where(diag, low, 0.0) for low in lows]
    d2s = [_bdot(d, d) for d in ds]
    d4s = [_bdot(d2, d2) for d2 in d2s]
    invs = [eye - d for d in ds]
    invs = [inv + _bdot(inv, d2) for inv, d2 in zip(invs, d2s)]
    invs = [inv + _bdot(inv, d4) for inv, d4 in zip(invs, d4s)]
    shift = 3
    while (1 << shift) < n:
        rb = row >> shift
        cb = col >> shift
        join = ((rb & 1) == 1) & (cb == rb - 1)
        inv16s = [inv.astype(BF16) for inv in invs]
        xs = [_bdot(jnp.where(join, low, 0.0), inv16) for low, inv16 in zip(lows, inv16s)]
        invs = [inv - _bdot(inv16, x) for inv, inv16, x in zip(invs, inv16s, xs)]
        shift += 1
    return invs


def _gdn_kernel(q_ref, k_ref, v_ref, z_ref, g_ref, gr_ref, cwq_ref, cwk_ref, cwv_ref, gain_ref, mix_ref,
                o_ref, state_ref, tail_ref, *, heads):
    del mix_ref
    hg = pl.program_id(1)
    t = pl.program_id(2)

    @pl.when(t == 0)
    def _():
        state_ref[...] = jnp.zeros_like(state_ref)
        tail_ref[...] = jnp.zeros_like(tail_ref)

    tb = q_ref.shape[0]
    c = GDN_CHUNK
    dk = HEAD_DIM

    def conv_silu(x_ref, w_ref, slot):
        assert GDN_CONV == 4
        x = x_ref[...].astype(F32)
        xe = jnp.concatenate([tail_ref[slot], x], axis=0)
        x1 = pltpu.roll(xe, 1, axis=0)
        w = w_ref[...]
        near = x * w[3:4, :] + x1[SUBLANES:, :] * w[2:3, :]
        far = xe * w[1:2, :] + x1 * w[0:1, :]
        y = near + pltpu.roll(far, 2, axis=0)[SUBLANES:, :]
        tail_ref[slot] = x[tb - SUBLANES:, :]
        return y * _sigmoid(y)

    def l2n(x):
        return x * lax.rsqrt(jnp.sum(x * x, axis=-1, keepdims=True) + EPS)

    q_raw = conv_silu(q_ref, cwq_ref, 0)
    k_raw = conv_silu(k_ref, cwk_ref, 1)
    v_raw = conv_silu(v_ref, cwv_ref, 2)
    gates = g_ref[...]
    lane = lax.broadcasted_iota(jnp.int32, gates.shape, 1)

    row = lax.broadcasted_iota(jnp.int32, (c, c), 0)
    col = lax.broadcasted_iota(jnp.int32, (c, c), 1)
    causal = row >= col
    strict = row > col
    last_lane = lax.broadcasted_iota(jnp.int32, (1, c), 1) == c - 1
    chunks = [slice(ci * c, (ci + 1) * c) for ci in range(tb // c)]

    hd = []
    for g in range(heads):
        cols = slice(g * dk, (g + 1) * dk)
        head = hg * heads + g
        q = l2n(q_raw[:, cols]) * (dk ** -0.5)
        k = l2n(k_raw[:, cols])
        beta = jnp.sum(jnp.where(lane == LANE_BETA + head, gates, 0.0), axis=1, keepdims=True)
        gcol = jnp.sum(jnp.where(lane == LANE_G + head, gates, 0.0), axis=1, keepdims=True)
        exp_g = jnp.exp(gcol)
        k_beta = k * beta
        hd.append(dict(cols=cols, q16=q.astype(BF16), k=k, k16=k.astype(BF16), kb16=k_beta.astype(BF16),
                       gcol=gcol, grow=gr_ref[g],
                       rhs=jnp.concatenate([v_raw[:, cols] * beta, k_beta * exp_g], axis=1).astype(BF16),
                       q_dec=q * exp_g))

    items = [(g, sl) for sl in chunks for g in range(heads)]
    decays = [jnp.exp(jnp.where(causal, hd[g]["gcol"][sl] - hd[g]["grow"][:, sl], -jnp.inf)) for g, sl in items]
    a_mats = [jnp.where(strict, lax.dot_general(hd[g]["kb16"][sl], hd[g]["k16"][sl], NT_DIMS,
                                                preferred_element_type=F32) * dec, 0.0)
              for (g, sl), dec in zip(items, decays)]
    intras = [jnp.where(causal, lax.dot_general(hd[g]["q16"][sl], hd[g]["k16"][sl], NT_DIMS,
                                                preferred_element_type=F32) * dec, 0.0).astype(BF16)
              for (g, sl), dec in zip(items, decays)]
    invs = _inv_unit_lower(a_mats, row, col)
    sols = [jnp.dot(inv.astype(BF16), hd[g]["rhs"][sl], preferred_element_type=F32)
            for (g, sl), inv in zip(items, invs)]

    states = [state_ref[g] for g in range(heads)]
    for ci, sl in enumerate(chunks):
        base = ci * heads
        g_lasts = [jnp.sum(jnp.where(last_lane, hd[g]["grow"][:, sl], 0.0), axis=1, keepdims=True)
                   for g in range(heads)]
        k_decs = [(hd[g]["k"][sl] * jnp.exp(g_lasts[g] - hd[g]["gcol"][sl])).astype(BF16) for g in range(heads)]
        wqs = [jnp.concatenate([sols[base + g][:, dk:], hd[g]["q_dec"][sl]], axis=0).astype(BF16)
               for g in range(heads)]
        wss = [jnp.dot(wqs[g], states[g].astype(BF16), preferred_element_type=F32) for g in range(heads)]
        vns = [(sols[base + g][:, :dk] - wss[g][:c]).astype(BF16) for g in range(heads)]
        outs = [wss[g][c:] + jnp.dot(intras[base + g], vns[g], preferred_element_type=F32) for g in range(heads)]
        states = [states[g] * jnp.exp(g_lasts[g])
                  + lax.dot_general(k_decs[g], vns[g], TN_DIMS, preferred_element_type=F32) for g in range(heads)]
        for g in range(heads):
            cols = hd[g]["cols"]
            z = z_ref[sl, cols].astype(F32)
            o = outs[g]
            o = o * lax.rsqrt(jnp.mean(o * o, axis=-1, keepdims=True) + EPS) * gain_ref[...]
            o_ref[sl, cols] = (o * (z * _sigmoid(z))).astype(o_ref.dtype)
    for g in range(heads):
        state_ref[g] = states[g]


def _gdn(big, gates, gates_t, conv_w, norm_gain, mix, layer, batch, seq, tb=512, heads=4):
    m = big.shape[0]
    nt = seq // tb
    width = heads * HEAD_DIM
    assert GDN_HEADS % heads == 0 and all(off % heads == 0 for off in (GDN_COL0, Z_COL0, MIX_GDN_COL0, LANE_G))
    gr = gates_t.reshape(GATE_ROWS, 1, m)
    est = 2 * (5 * tb * width * 2 + tb * LANES * 4 + heads * tb * 32) + 64 * tb * width * 4
    blk = lambda off: pl.BlockSpec((tb, width), lambda b, h, t: (b * nt + t, off // heads + h))
    cw = lambda off: pl.BlockSpec((None, GDN_CONV, width), lambda b, h, t: (layer, 0, off // heads + h))
    return pl.pallas_call(
        functools.partial(_gdn_kernel, heads=heads),
        grid=(batch, GDN_HEADS // heads, nt),
        in_specs=[
            blk(GDN_COL0), blk(GDN_COL0 + GDN_HEADS), blk(GDN_COL0 + 2 * GDN_HEADS), blk(Z_COL0),
            pl.BlockSpec((tb, LANES), lambda b, h, t: (b * nt + t, 0)),
            pl.BlockSpec((heads, 1, tb), lambda b, h, t: (LANE_G // heads + h, 0, b * nt + t)),
            cw(0), cw(GDN_HEADS), cw(2 * GDN_HEADS),
            pl.BlockSpec((None, 1, HEAD_DIM), lambda b, h, t: (layer, 0, 0)),
            pl.BlockSpec(memory_space=pl.ANY),
        ],
        out_specs=pl.BlockSpec((tb, width), lambda b, h, t: (b * nt + t, MIX_GDN_COL0 // heads + h)),
        out_shape=jax.ShapeDtypeStruct(mix.shape, mix.dtype),
        input_output_aliases={10: 0},
        scratch_shapes=[pltpu.VMEM((heads, HEAD_DIM, HEAD_DIM), F32),
                        pltpu.VMEM((3, SUBLANES, width), F32)],
        compiler_params=_params(("arbitrary", "arbitrary", "arbitrary"), est),
        name="gated_delta_rule",
    )(big, big, big, big, gates, gr, conv_w, conv_w, conv_w,
      norm_gain.reshape(norm_gain.shape[0], 1, HEAD_DIM), mix)


def _ffn_up_kernel(h_ref, wg_ref, wu_ref, cg_ref, cu_ref, cast_in_ref, o_ref, cast_out_ref, tail_ref, *,
                   blocks_per_seq):
    i = pl.program_id(0)
    j = pl.program_id(1)
    tm = h_ref.shape[0]

    @pl.when(i % blocks_per_seq == 0)
    def _():
        tail_ref[j] = jnp.zeros(tail_ref.shape[1:], F32)

    h = h_ref[...]
    yg = jnp.dot(h, wg_ref[...], preferred_element_type=F32)
    yu = jnp.dot(h, wu_ref[...], preferred_element_type=F32)
    tail = tail_ref[j]

    def conv(y, prev, w):
        out = y * w[FFN_CONV - 1:FFN_CONV, :]
        for back in range(1, FFN_CONV):
            out = out + _shift_rows(y, prev, back) * w[FFN_CONV - 1 - back:FFN_CONV - back, :]
        return out

    ug = conv(yg, tail[:SUBLANES], cg_ref[...])
    uu = conv(yu, tail[SUBLANES:], cu_ref[...])
    tail_ref[j] = jnp.concatenate([yg[tm - SUBLANES:], yu[tm - SUBLANES:]], axis=0)
    o_ref[...] = (ug * _sigmoid(ug) * uu).astype(o_ref.dtype)
    cast_out_ref[...] = cast_in_ref[...].astype(BF16)


def _ffn_up(h, w_up, conv_w, cast_src, layer, seq, tm=1024, tn=256):
    m, d = h.shape
    nj = FFN_DIM // tn
    est = 2 * (tm * d * 2 + 2 * d * tn * 2 + tm * tn * 2) + nj * 2 * SUBLANES * tn * 4 + 12 * tm * tn * 4
    c_in, c_out, c_shapes, c_vmem = _cast_specs([cast_src], layer, (m // tm) * nj, lambda i, j: i * nj + j)
    return pl.pallas_call(
        functools.partial(_ffn_up_kernel, blocks_per_seq=seq // tm),
        grid=(m // tm, nj),
        in_specs=[pl.BlockSpec((tm, d), lambda i, j: (i, 0)),
                  pl.BlockSpec((None, d, tn), lambda i, j: (0, 0, j)),
                  pl.BlockSpec((None, d, tn), lambda i, j: (0, 0, nj + j)),
                  pl.BlockSpec((None, FFN_CONV, tn), lambda i, j: (layer, 0, j)),
                  pl.BlockSpec((None, FFN_CONV, tn), lambda i, j: (layer, 0, nj + j))] + c_in,
        out_specs=[pl.BlockSpec((tm, tn), lambda i, j: (i, j))] + c_out,
        out_shape=[jax.ShapeDtypeStruct((m, FFN_DIM), BF16)] + c_shapes,
        scratch_shapes=[pltpu.VMEM((nj, 2 * SUBLANES, tn), F32)],
        compiler_params=_params(("arbitrary", "arbitrary"), est + c_vmem),
        name="ffn_up_conv_gate",
    )(h, w_up, w_up, conv_w, conv_w, cast_src)


def kernel(x, norm_mix_gain, w_in, pool_w, pool_scale, fox_f_bias, gdn_conv_w, gdn_A_log, gdn_dt_bias,
           gdn_norm_gain, w_o, norm_ffn_gain, w_up, ffn_conv_w, w_down, final_norm_gain):
    batch, seq, d = x.shape
    n_layers = norm_mix_gain.shape[0]
    assert d == D_MODEL and w_in.shape[2] == IN_DIM and seq % 1024 == 0
    w_big, w_small = _regroup_w_in(w_in)
    pool_w16 = pool_w.astype(BF16)
    zeros_h = jnp.zeros((n_layers, GDN_HEADS), F32)
    zeros_pad = jnp.zeros((n_layers, LANES - N_GATES), F32)
    gate_bias = jnp.concatenate([fox_f_bias.astype(F32), zeros_h, gdn_dt_bias.astype(F32), zeros_pad], axis=1)
    gate_alog = jnp.concatenate([jnp.zeros((n_layers, FOX_HEADS), F32), zeros_h, gdn_A_log.astype(F32), zeros_pad],
                                axis=1)

    xf = x.reshape(batch * seq, d).astype(F32)
    for l in range(n_layers):
        h = _rmsnorm(xf, norm_mix_gain[l], BF16)
        big, w_o16, w_up16 = _matmul(h, w_big, l, BF16, tm=1024, tn=512, casts=(w_o, w_up), name="in_proj")
        small = _matmul(h, w_small, l, F32, tm=1024, tn=LANES, name="gate_proj")
        gates, gates_t = _gates(small, gate_bias[l], gate_alog[l], batch, seq)
        mix = _pool(big, pool_w16, pool_scale, l, batch, seq)
        mix = _fox_attention(big, gates, gates_t, mix, batch, seq)
        mix = _gdn(big, gates, gates_t, gdn_conv_w, gdn_norm_gain, mix, l, batch, seq)
        xf = _matmul(mix, w_o16, 0, F32, tm=1024, tn=1024, residual=xf, name="out_proj")
        h = _rmsnorm(xf, norm_ffn_gain[l], BF16)
        act, w_down16 = _ffn_up(h, w_up16, ffn_conv_w, w_down, l, seq)
        xf = _matmul(act, w_down16, 0, F32, tm=512, tn=512, residual=xf, name="ffn_down")
    out = _rmsnorm(xf, final_norm_gain, x.dtype)
    return out.reshape(batch, seq, d)
```

```python
import functools
import math

import jax
import jax.numpy as jnp
from jax import lax
from jax.experimental import pallas as pl
from jax.experimental.pallas import tpu as pltpu

D_MODEL = 4096
HEAD_DIM = 128
POOL_WINDOWS = (2, 4, 8, 16)
POOL_GROUPS = 4
POOL_GROUP_DIM = D_MODEL // 16
POOL_DIM = POOL_GROUPS * POOL_GROUP_DIM
ATTN_DIM = (D_MODEL - POOL_DIM) // 2
FOX_HEADS = ATTN_DIM // HEAD_DIM
GDN_DIM = D_MODEL - POOL_DIM - ATTN_DIM
GDN_HEADS = GDN_DIM // HEAD_DIM
GDN_CONV = 4
FFN_DIM = 11008
FFN_CONV = 3
EPS = 1e-6
IN_DIM = POOL_DIM + 3 * ATTN_DIM + FOX_HEADS + 3 * GDN_DIM + GDN_DIM + 2 * GDN_HEADS

LANES = 128
SUBLANES = 8
VMEM_BYTES_V7X = 64 * 1024 * 1024
VMEM_CAP = VMEM_BYTES_V7X - 8 * 1024 * 1024

ALIGNED_DIM = POOL_DIM + 3 * ATTN_DIM
BIG_DIM = ALIGNED_DIM + 3 * GDN_DIM + GDN_DIM
N_GATES = FOX_HEADS + 2 * GDN_HEADS
FOX_COL0 = POOL_DIM // LANES
GDN_COL0 = ALIGNED_DIM // LANES
Z_COL0 = (ALIGNED_DIM + 3 * GDN_DIM) // LANES
MIX_FOX_COL0 = POOL_DIM // LANES
MIX_GDN_COL0 = (POOL_DIM + ATTN_DIM) // LANES
LANE_F = 0
LANE_BETA = FOX_HEADS
LANE_G = FOX_HEADS + GDN_HEADS
GATE_ROWS = 48
GDN_CHUNK = 128

MXU_COLS = 256
ROW_BLOCK = 512
FOX_BLOCK = 512
FOX_HEADS_PER_STEP = 2
GDN_HEADS_PER_STEP = 4
IN_PROJ_TILE = (1024, 512)
OUT_PROJ_TILE = (1024, 1024)
FFN_UP_TILE = (1024, MXU_COLS)
FFN_DOWN_TILE = (512, 512)
POOL_HALO = 16
INV_BASE_BITS = 3

F32 = jnp.float32
BF16 = jnp.bfloat16
NT_DIMS = (((1,), (1,)), ((), ()))
TN_DIMS = (((0,), (0,)), ((), ()))


def _params(semantics, vmem_estimate):
    limit = min(int(vmem_estimate * 1.25) + (4 << 20), VMEM_CAP)
    return pltpu.CompilerParams(dimension_semantics=semantics, vmem_limit_bytes=limit)


def _sigmoid(x):
    return 1.0 / (1.0 + jnp.exp(-x))


def _softplus(x):
    return jnp.maximum(x, 0.0) + jnp.log1p(jnp.exp(-jnp.abs(x)))


def _shift_rows(x, prev, k):
    n, w = x.shape
    x3 = jnp.concatenate([prev, x], axis=0).reshape(n // SUBLANES + 1, SUBLANES, w)
    r = pltpu.roll(x3, k, axis=1)
    sub = lax.broadcasted_iota(jnp.int32, (n // SUBLANES, SUBLANES, w), 1)
    return jnp.where(sub < k, r[:-1], r[1:]).reshape(n, w)


REGROUP_COLS = 256
REGROUP_HALO = 16


def _regroup_kernel(cur_ref, nxt_ref, big_ref):
    c = pl.program_id(0)
    n_layers = cur_ref.shape[1]

    def emit(src):
        for l in range(n_layers):
            big_ref[l] = src[:, l, :].T.astype(BF16)

    @pl.when(c < ALIGNED_DIM // REGROUP_COLS)
    def _():
        emit(cur_ref[...])

    @pl.when(c >= ALIGNED_DIM // REGROUP_COLS)
    def _():
        emit(jnp.concatenate([cur_ref[FOX_HEADS:], nxt_ref[:FOX_HEADS]], axis=0))


def _gate_cols_kernel(src_ref, small_ref):
    for l in range(src_ref.shape[1]):
        small_ref[l] = src_ref[:, l, :].T.astype(BF16)


def _regroup_w_in(w_in):
    n_layers, d, _ = w_in.shape
    assert ALIGNED_DIM % REGROUP_COLS == 0 and BIG_DIM % REGROUP_COLS == 0 and FOX_HEADS <= REGROUP_HALO
    wt = jnp.transpose(w_in, (2, 0, 1))
    halo_blocks = REGROUP_COLS // REGROUP_HALO
    est = 2 * (REGROUP_COLS + REGROUP_HALO) * n_layers * d * 4 + 2 * n_layers * d * REGROUP_COLS * 2 \
        + 4 * REGROUP_COLS * n_layers * d * 4
    big = pl.pallas_call(
        _regroup_kernel,
        grid=(BIG_DIM // REGROUP_COLS,),
        in_specs=[pl.BlockSpec((REGROUP_COLS, n_layers, d), lambda c: (c, 0, 0)),
                  pl.BlockSpec((REGROUP_HALO, n_layers, d), lambda c: ((c + 1) * halo_blocks, 0, 0))],
        out_specs=pl.BlockSpec((n_layers, d, REGROUP_COLS), lambda c: (0, 0, c)),
        out_shape=jax.ShapeDtypeStruct((n_layers, d, BIG_DIM), BF16),
        compiler_params=_params(("arbitrary",), est),
        name="regroup_w_in",
    )(wt, wt)
    gate_src = jnp.concatenate([wt[ALIGNED_DIM:ALIGNED_DIM + FOX_HEADS], wt[IN_DIM - 2 * GDN_HEADS:],
                                jnp.zeros((LANES - N_GATES, n_layers, d), w_in.dtype)], axis=0)
    small = pl.pallas_call(
        _gate_cols_kernel,
        out_shape=jax.ShapeDtypeStruct((n_layers, d, LANES), BF16),
        name="regroup_gate_cols",
    )(gate_src)
    return big, small


def _rms_kernel(x_ref, g_ref, o_ref):
    x = x_ref[...]
    ms = jnp.mean(x * x, axis=-1, keepdims=True)
    o_ref[...] = (x * lax.rsqrt(ms + EPS) * g_ref[...]).astype(o_ref.dtype)


def _rmsnorm(x, gain, out_dtype, tm=ROW_BLOCK):
    m, d = x.shape
    est = 2 * tm * d * (4 + jnp.dtype(out_dtype).itemsize)
    return pl.pallas_call(
        _rms_kernel,
        grid=(m // tm,),
        in_specs=[pl.BlockSpec((tm, d), lambda i: (i, 0)),
                  pl.BlockSpec((1, d), lambda i: (0, 0))],
        out_specs=pl.BlockSpec((tm, d), lambda i: (i, 0)),
        out_shape=jax.ShapeDtypeStruct((m, d), out_dtype),
        compiler_params=_params(("arbitrary",), est),
        name="rmsnorm",
    )(x, gain.reshape(1, d))


BF16_TILE_ROWS = 16


def _cast_specs(srcs, layer, steps, step_of):
    in_specs, out_specs, out_shapes, vmem = [], [], [], 0
    for src in srcs:
        _, rows, cols = src.shape
        rb = next(r for r in range(BF16_TILE_ROWS, rows + 1, BF16_TILE_ROWS) if rows % r == 0 and rows // r <= steps)
        blk = lambda *idx, n_blocks=rows // rb: jnp.minimum(step_of(*idx), n_blocks - 1)
        in_specs.append(pl.BlockSpec((None, rb, cols), lambda *idx, blk=blk: (layer, blk(*idx), 0)))
        out_specs.append(pl.BlockSpec((None, rb, cols), lambda *idx, blk=blk: (0, blk(*idx), 0)))
        out_shapes.append(jax.ShapeDtypeStruct((1, rows, cols), BF16))
        vmem += 2 * rb * cols * (4 + 2)
    return in_specs, out_specs, out_shapes, vmem


def _mm_kernel(*refs, has_res, n_casts):
    a_ref, b_ref = refs[0], refs[1]
    n_in = 2 + int(has_res) + n_casts
    o_ref = refs[n_in]
    acc = jnp.dot(a_ref[...], b_ref[...], preferred_element_type=F32)
    if has_res:
        acc = acc + refs[2][...]
    o_ref[...] = acc.astype(o_ref.dtype)
    for c in range(n_casts):
        refs[n_in + 1 + c][...] = refs[n_in - n_casts + c][...].astype(BF16)


def _matmul(a, w, layer, out_dtype, tm, tn, residual=None, casts=(), name="matmul"):
    m, kdim = a.shape
    n = w.shape[2]
    has_res = residual is not None
    nj = n // tn
    in_specs = [pl.BlockSpec((tm, kdim), lambda i, j: (i, 0)),
                pl.BlockSpec((None, kdim, tn), lambda i, j: (layer, 0, j))]
    args = [a, w]
    est = 2 * tm * kdim * a.dtype.itemsize + 2 * kdim * tn * w.dtype.itemsize
    if has_res:
        in_specs.append(pl.BlockSpec((tm, tn), lambda i, j: (i, j)))
        est += 2 * tm * tn * 4
        args.append(residual)
    est += 2 * tm * tn * jnp.dtype(out_dtype).itemsize + 2 * tm * tn * 4
    c_in, c_out, c_shapes, c_vmem = _cast_specs(casts, layer, (m // tm) * nj, lambda i, j: i * nj + j)
    out = pl.pallas_call(
        functools.partial(_mm_kernel, has_res=has_res, n_casts=len(casts)),
        grid=(m // tm, nj),
        in_specs=in_specs + c_in,
        out_specs=[pl.BlockSpec((tm, tn), lambda i, j: (i, j))] + c_out,
        out_shape=[jax.ShapeDtypeStruct((m, n), out_dtype)] + c_shapes,
        compiler_params=_params(("arbitrary", "arbitrary"), est + c_vmem),
        name=name,
    )(*args, *casts)
    return out[0] if not casts else tuple(out)


def _scan_rows(y, row, seg):
    pos = row & (seg - 1)
    s = 1
    while s < seg:
        y = y + jnp.where(pos >= s, pltpu.roll(y, s, axis=0), 0.0)
        s *= 2
    return y


def _gates_kernel(x_ref, p_ref, g_ref, gt_ref, carry_ref):
    t = pl.program_id(1)

    @pl.when(t == 0)
    def _():
        carry_ref[...] = jnp.zeros_like(carry_ref)

    tb = x_ref.shape[0]
    z = x_ref[...] + p_ref[0:1, :]
    lane = lax.broadcasted_iota(jnp.int32, z.shape, 1)
    row = lax.broadcasted_iota(jnp.int32, z.shape, 0)
    log_f = -_softplus(-z)
    beta = _sigmoid(z)
    g = -jnp.exp(p_ref[1:2, :]) * _softplus(z)
    cum_f = _scan_rows(log_f, row, tb) + carry_ref[0:1, :]
    carry_ref[0:1, :] = cum_f[tb - 1:tb, :]
    cum_g = _scan_rows(g, row, GDN_CHUNK)
    out = jnp.where(lane < LANE_BETA, cum_f, jnp.where(lane < LANE_G, beta, cum_g))
    g_ref[...] = out
    gt_ref[...] = out.T[:GATE_ROWS, :]


def _gates(small, bias_row, alog_row, batch, seq, tb=ROW_BLOCK):
    m = small.shape[0]
    nt = seq // tb
    params = jnp.zeros((SUBLANES, LANES), F32).at[0].set(bias_row).at[1].set(alog_row)
    est = 2 * tb * LANES * 4 * 3 + 16 * tb * LANES * 4
    return pl.pallas_call(
        _gates_kernel,
        grid=(batch, nt),
        in_specs=[pl.BlockSpec((tb, LANES), lambda b, t: (b * nt + t, 0)),
                  pl.BlockSpec((SUBLANES, LANES), lambda b, t: (0, 0))],
        out_specs=[pl.BlockSpec((tb, LANES), lambda b, t: (b * nt + t, 0)),
                   pl.BlockSpec((GATE_ROWS, tb), lambda b, t: (0, b * nt + t))],
        out_shape=[jax.ShapeDtypeStruct((m, LANES), F32),
                   jax.ShapeDtypeStruct((GATE_ROWS, m), F32)],
        scratch_shapes=[pltpu.VMEM((SUBLANES, LANES), F32)],
        compiler_params=_params(("arbitrary", "arbitrary"), est),
        name="gates",
    )(small, params)


def _pool_kernel(x_ref, w_ref, sc_ref, o_ref, tail_ref):
    t = pl.program_id(1)

    @pl.when(t == 0)
    def _():
        tail_ref[...] = jnp.zeros_like(tail_ref)

    tb = x_ref.shape[0]
    halo = tail_ref.shape[0]
    cg = POOL_GROUP_DIM
    x = x_ref[...].astype(F32)
    xe = jnp.concatenate([tail_ref[...], x], axis=0)
    tail_ref[...] = x[tb - halo:, :]
    pos = (t * tb + 1 + lax.broadcasted_iota(jnp.int32, (tb, cg), 0)).astype(F32)
    for gi, win in enumerate(POOL_WINDOWS):
        s = xe[:, gi * cg:(gi + 1) * cg]
        span = 1
        while span < win:
            s = s + pltpu.roll(s, span, axis=0)
            span *= 2
        mean = s[halo:, :] / jnp.minimum(pos, float(win))
        pooled = (mean - x[:, gi * cg:(gi + 1) * cg]).astype(BF16)
        y = jnp.dot(pooled, w_ref[gi], preferred_element_type=F32)
        o_ref[:, gi * cg:(gi + 1) * cg] = (y * sc_ref[:, gi * cg:(gi + 1) * cg]).astype(o_ref.dtype)
    o_ref[:, POOL_DIM:] = jnp.zeros((tb, o_ref.shape[1] - POOL_DIM), o_ref.dtype)


def _pool(big, pool_w, pool_scale, layer, batch, seq, tb=ROW_BLOCK):
    m = big.shape[0]
    nt = seq // tb
    halo = POOL_HALO
    assert halo >= max(POOL_WINDOWS) and halo % SUBLANES == 0
    est = 2 * tb * (POOL_DIM + D_MODEL) * 2 + 8 * tb * POOL_DIM * 4
    return pl.pallas_call(
        _pool_kernel,
        grid=(batch, nt),
        in_specs=[pl.BlockSpec((tb, POOL_DIM), lambda b, t: (b * nt + t, 0)),
                  pl.BlockSpec((None, POOL_GROUPS, POOL_GROUP_DIM, POOL_GROUP_DIM), lambda b, t: (layer, 0, 0, 0)),
                  pl.BlockSpec((None, 1, POOL_DIM), lambda b, t: (layer, 0, 0))],
        out_specs=pl.BlockSpec((tb, D_MODEL), lambda b, t: (b * nt + t, 0)),
        out_shape=jax.ShapeDtypeStruct((m, D_MODEL), BF16),
        scratch_shapes=[pltpu.VMEM((halo, POOL_DIM), F32)],
        compiler_params=_params(("arbitrary", "arbitrary"), est),
        name="pool_mixer",
    )(big, pool_w, pool_scale.reshape(pool_scale.shape[0], 1, POOL_DIM))


def _fox_kernel(q_ref, k_ref, v_ref, g_ref, cr_ref, mix_ref, o_ref, *, tq, heads):
    del mix_ref
    hg = pl.program_id(1)
    qi = pl.program_id(2)
    log2e = math.log2(math.e)
    scale2 = log2e / math.sqrt(HEAD_DIM)
    dh = HEAD_DIM
    gates = g_ref[...]
    lane = lax.broadcasted_iota(jnp.int32, gates.shape, 1)
    qs = [q_ref[:, g * dh:(g + 1) * dh] for g in range(heads)]
    cqs = [jnp.sum(jnp.where(lane == LANE_F + hg * heads + g, gates, 0.0), axis=1, keepdims=True) * log2e
           for g in range(heads)]
    tri = (lax.broadcasted_iota(jnp.int32, (tq, tq), 0) >= lax.broadcasted_iota(jnp.int32, (tq, tq), 1))

    def step(ki, carry, masked):
        start = pl.multiple_of(ki * tq, tq)
        kbs = [k_ref[pl.ds(start, tq), g * dh:(g + 1) * dh] for g in range(heads)]
        vbs = [v_ref[pl.ds(start, tq), g * dh:(g + 1) * dh] for g in range(heads)]
        ss = [lax.dot_general(qs[g], kbs[g], NT_DIMS, preferred_element_type=F32) for g in range(heads)]
        zps = [ss[g] * scale2 - cr_ref[g, ki] * log2e for g in range(heads)]
        if masked:
            zps = [jnp.where(tri, zp, -jnp.inf) for zp in zps]
        m_news = [jnp.maximum(carry[g][0], jnp.max(zps[g], axis=1, keepdims=True) + cqs[g]) for g in range(heads)]
        ps = [jnp.exp2(zps[g] - (m_news[g] - cqs[g])) for g in range(heads)]
        alphas = [jnp.exp2(carry[g][0] - m_news[g]) for g in range(heads)]
        l_news = [alphas[g] * carry[g][1] + jnp.sum(ps[g], axis=1, keepdims=True) for g in range(heads)]
        accs = [alphas[g] * carry[g][2] + jnp.dot(ps[g].astype(BF16), vbs[g], preferred_element_type=F32)
                for g in range(heads)]
        return tuple((m_news[g], l_news[g], accs[g]) for g in range(heads))

    init = tuple((jnp.full((tq, 1), -jnp.inf, F32), jnp.zeros((tq, 1), F32), jnp.zeros((tq, dh), F32))
                 for _ in range(heads))
    carry = lax.fori_loop(0, qi, lambda ki, c: step(ki, c, False), init)
    final = step(qi, carry, True)
    for g in range(heads):
        _, l_fin, acc = final[g]
        o_ref[:, g * dh:(g + 1) * dh] = (acc / l_fin).astype(o_ref.dtype)


def _fox_attention(big, gates, gates_t, mix, batch, seq, tq=FOX_BLOCK, heads=FOX_HEADS_PER_STEP):
    m = big.shape[0]
    nq = seq // tq
    width = heads * HEAD_DIM
    assert FOX_HEADS % heads == 0 and FOX_COL0 % heads == 0 and MIX_FOX_COL0 % heads == 0 and LANE_F % heads == 0
    cr = gates_t.reshape(GATE_ROWS, m // tq, 1, tq)
    est = (2 * (tq * width * 2 * 2 + 2 * seq * width * 2 + tq * LANES * 4 + heads * seq * 4 * 8)
           + heads * 10 * tq * tq * 4)
    return pl.pallas_call(
        functools.partial(_fox_kernel, tq=tq, heads=heads),
        grid=(batch, FOX_HEADS // heads, nq),
        in_specs=[
            pl.BlockSpec((tq, width), lambda b, h, i: (b * nq + i, FOX_COL0 // heads + h)),
            pl.BlockSpec((seq, width), lambda b, h, i: (b, (FOX_COL0 + FOX_HEADS) // heads + h)),
            pl.BlockSpec((seq, width), lambda b, h, i: (b, (FOX_COL0 + 2 * FOX_HEADS) // heads + h)),
            pl.BlockSpec((tq, LANES), lambda b, h, i: (b * nq + i, 0)),
            pl.BlockSpec((heads, nq, 1, tq), lambda b, h, i: (LANE_F // heads + h, b, 0, 0)),
            pl.BlockSpec(memory_space=pl.ANY),
        ],
        out_specs=pl.BlockSpec((tq, width), lambda b, h, i: (b * nq + i, MIX_FOX_COL0 // heads + h)),
        out_shape=jax.ShapeDtypeStruct(mix.shape, mix.dtype),
        input_output_aliases={5: 0},
        compiler_params=_params(("arbitrary", "arbitrary", "arbitrary"), est),
        name="fox_attention",
    )(big, big, big, gates, cr, mix)


def _bdot(a, b):
    return jnp.dot(a.astype(BF16), b.astype(BF16), preferred_element_type=F32)


def _inv_unit_lower(lows, row, col):
    n = lows[0].shape[0]
    eye = (row == col).astype(F32)
    assert INV_BASE_BITS == 3
    diag = (row >> INV_BASE_BITS) == (col >> INV_BASE_BITS)
    ds = [jnp.where(diag, low, 0.0) for low in lows]
    d2s = [_bdot(d, d) for d in ds]
    d4s = [_bdot(d2, d2) for d2 in d2s]
    invs = [eye - d for d in ds]
    invs = [inv + _bdot(inv, d2) for inv, d2 in zip(invs, d2s)]
    invs = [inv + _bdot(inv, d4) for inv, d4 in zip(invs, d4s)]
    shift = INV_BASE_BITS
    while (1 << shift) < n:
        rb = row >> shift
        cb = col >> shift
        join = ((rb & 1) == 1) & (cb == rb - 1)
        inv16s = [inv.astype(BF16) for inv in invs]
        xs = [_bdot(jnp.where(join, low, 0.0), inv16) for low, inv16 in zip(lows, inv16s)]
        invs = [inv - _bdot(inv16, x) for inv, inv16, x in zip(invs, inv16s, xs)]
        shift += 1
    return invs


def _gdn_kernel(q_ref, k_ref, v_ref, z_ref, g_ref, gr_ref, cwq_ref, cwk_ref, cwv_ref, gain_ref, mix_ref,
                o_ref, state_ref, tail_ref, *, heads):
    del mix_ref
    hg = pl.program_id(1)
    t = pl.program_id(2)

    @pl.when(t == 0)
    def _():
        state_ref[...] = jnp.zeros_like(state_ref)
        tail_ref[...] = jnp.zeros_like(tail_ref)

    tb = q_ref.shape[0]
    c = GDN_CHUNK
    dk = HEAD_DIM

    def conv_silu(x_ref, w_ref, slot):
        assert GDN_CONV == 4
        x = x_ref[...].astype(F32)
        xe = jnp.concatenate([tail_ref[slot], x], axis=0)
        x1 = pltpu.roll(xe, 1, axis=0)
        w = w_ref[...]
        near = x * w[3:4, :] + x1[SUBLANES:, :] * w[2:3, :]
        far = xe * w[1:2, :] + x1 * w[0:1, :]
        y = near + pltpu.roll(far, 2, axis=0)[SUBLANES:, :]
        tail_ref[slot] = x[tb - SUBLANES:, :]
        return y * _sigmoid(y)

    def l2n(x):
        return x * lax.rsqrt(jnp.sum(x * x, axis=-1, keepdims=True) + EPS)

    q_raw = conv_silu(q_ref, cwq_ref, 0)
    k_raw = conv_silu(k_ref, cwk_ref, 1)
    v_raw = conv_silu(v_ref, cwv_ref, 2)
    gates = g_ref[...]
    lane = lax.broadcasted_iota(jnp.int32, gates.shape, 1)

    row = lax.broadcasted_iota(jnp.int32, (c, c), 0)
    col = lax.broadcasted_iota(jnp.int32, (c, c), 1)
    causal = row >= col
    strict = row > col
    last_lane = lax.broadcasted_iota(jnp.int32, (1, c), 1) == c - 1
    chunks = [slice(ci * c, (ci + 1) * c) for ci in range(tb // c)]

    hd = []
    for g in range(heads):
        cols = slice(g * dk, (g + 1) * dk)
        head = hg * heads + g
        q = l2n(q_raw[:, cols]) * (dk ** -0.5)
        k = l2n(k_raw[:, cols])
        beta = jnp.sum(jnp.where(lane == LANE_BETA + head, gates, 0.0), axis=1, keepdims=True)
        gcol = jnp.sum(jnp.where(lane == LANE_G + head, gates, 0.0), axis=1, keepdims=True)
        exp_g = jnp.exp(gcol)
        k_beta = k * beta
        hd.append(dict(cols=cols, q16=q.astype(BF16), k=k, k16=k.astype(BF16), kb16=k_beta.astype(BF16),
                       gcol=gcol, grow=gr_ref[g],
                       rhs=jnp.concatenate([v_raw[:, cols] * beta, k_beta * exp_g], axis=1).astype(BF16),
                       q_dec=q * exp_g))

    items = [(g, sl) for sl in chunks for g in range(heads)]
    decays = [jnp.exp(jnp.where(causal, hd[g]["gcol"][sl] - hd[g]["grow"][:, sl], -jnp.inf)) for g, sl in items]
    a_mats = [jnp.where(strict, lax.dot_general(hd[g]["kb16"][sl], hd[g]["k16"][sl], NT_DIMS,
                                                preferred_element_type=F32) * dec, 0.0)
              for (g, sl), dec in zip(items, decays)]
    intras = [jnp.where(causal, lax.dot_general(hd[g]["q16"][sl], hd[g]["k16"][sl], NT_DIMS,
                                                preferred_element_type=F32) * dec, 0.0).astype(BF16)
              for (g, sl), dec in zip(items, decays)]
    invs = _inv_unit_lower(a_mats, row, col)
    sols = [jnp.dot(inv.astype(BF16), hd[g]["rhs"][sl], preferred_element_type=F32)
            for (g, sl), inv in zip(items, invs)]

    states = [state_ref[g] for g in range(heads)]
    for ci, sl in enumerate(chunks):
        base = ci * heads
        g_lasts = [jnp.sum(jnp.where(last_lane, hd[g]["grow"][:, sl], 0.0), axis=1, keepdims=True)
                   for g in range(heads)]
        k_decs = [(hd[g]["k"][sl] * jnp.exp(g_lasts[g] - hd[g]["gcol"][sl])).astype(BF16) for g in range(heads)]
        wqs = [jnp.concatenate([sols[base + g][:, dk:], hd[g]["q_dec"][sl]], axis=0).astype(BF16)
               for g in range(heads)]
        wss = [jnp.dot(wqs[g], states[g].astype(BF16), preferred_element_type=F32) for g in range(heads)]
        vns = [(sols[base + g][:, :dk] - wss[g][:c]).astype(BF16) for g in range(heads)]
        outs = [wss[g][c:] + jnp.dot(intras[base + g], vns[g], preferred_element_type=F32) for g in range(heads)]
        states = [states[g] * jnp.exp(g_lasts[g])
                  + lax.dot_general(k_decs[g], vns[g], TN_DIMS, preferred_element_type=F32) for g in range(heads)]
        for g in range(heads):
            cols = hd[g]["cols"]
            z = z_ref[sl, cols].astype(F32)
            o = outs[g]
            o = o * lax.rsqrt(jnp.mean(o * o, axis=-1, keepdims=True) + EPS) * gain_ref[...]
            o_ref[sl, cols] = (o * (z * _sigmoid(z))).astype(o_ref.dtype)
    for g in range(heads):
        state_ref[g] = states[g]


def _gdn(big, gates, gates_t, conv_w, norm_gain, mix, layer, batch, seq, tb=ROW_BLOCK, heads=GDN_HEADS_PER_STEP):
    m = big.shape[0]
    nt = seq // tb
    width = heads * HEAD_DIM
    assert GDN_HEADS % heads == 0 and all(off % heads == 0 for off in (GDN_COL0, Z_COL0, MIX_GDN_COL0, LANE_G))
    gr = gates_t.reshape(GATE_ROWS, 1, m)
    est = 2 * (5 * tb * width * 2 + tb * LANES * 4 + heads * tb * 32) + 64 * tb * width * 4
    blk = lambda off: pl.BlockSpec((tb, width), lambda b, h, t: (b * nt + t, off // heads + h))
    cw = lambda off: pl.BlockSpec((None, GDN_CONV, width), lambda b, h, t: (layer, 0, off // heads + h))
    return pl.pallas_call(
        functools.partial(_gdn_kernel, heads=heads),
        grid=(batch, GDN_HEADS // heads, nt),
        in_specs=[
            blk(GDN_COL0), blk(GDN_COL0 + GDN_HEADS), blk(GDN_COL0 + 2 * GDN_HEADS), blk(Z_COL0),
            pl.BlockSpec((tb, LANES), lambda b, h, t: (b * nt + t, 0)),
            pl.BlockSpec((heads, 1, tb), lambda b, h, t: (LANE_G // heads + h, 0, b * nt + t)),
            cw(0), cw(GDN_HEADS), cw(2 * GDN_HEADS),
            pl.BlockSpec((None, 1, HEAD_DIM), lambda b, h, t: (layer, 0, 0)),
            pl.BlockSpec(memory_space=pl.ANY),
        ],
        out_specs=pl.BlockSpec((tb, width), lambda b, h, t: (b * nt + t, MIX_GDN_COL0 // heads + h)),
        out_shape=jax.ShapeDtypeStruct(mix.shape, mix.dtype),
        input_output_aliases={10: 0},
        scratch_shapes=[pltpu.VMEM((heads, HEAD_DIM, HEAD_DIM), F32),
                        pltpu.VMEM((3, SUBLANES, width), F32)],
        compiler_params=_params(("arbitrary", "arbitrary", "arbitrary"), est),
        name="gated_delta_rule",
    )(big, big, big, big, gates, gr, conv_w, conv_w, conv_w,
      norm_gain.reshape(norm_gain.shape[0], 1, HEAD_DIM), mix)


def _ffn_up_kernel(h_ref, wg_ref, wu_ref, cg_ref, cu_ref, cast_in_ref, o_ref, cast_out_ref, tail_ref, *,
                   blocks_per_seq):
    i = pl.program_id(0)
    j = pl.program_id(1)
    tm = h_ref.shape[0]

    @pl.when(i % blocks_per_seq == 0)
    def _():
        tail_ref[j] = jnp.zeros(tail_ref.shape[1:], F32)

    h = h_ref[...]
    yg = jnp.dot(h, wg_ref[...], preferred_element_type=F32)
    yu = jnp.dot(h, wu_ref[...], preferred_element_type=F32)
    tail = tail_ref[j]

    def conv(y, prev, w):
        out = y * w[FFN_CONV - 1:FFN_CONV, :]
        for back in range(1, FFN_CONV):
            out = out + _shift_rows(y, prev, back) * w[FFN_CONV - 1 - back:FFN_CONV - back, :]
        return out

    ug = conv(yg, tail[:SUBLANES], cg_ref[...])
    uu = conv(yu, tail[SUBLANES:], cu_ref[...])
    tail_ref[j] = jnp.concatenate([yg[tm - SUBLANES:], yu[tm - SUBLANES:]], axis=0)
    o_ref[...] = (ug * _sigmoid(ug) * uu).astype(o_ref.dtype)
    cast_out_ref[...] = cast_in_ref[...].astype(BF16)


def _ffn_up(h, w_up, conv_w, cast_src, layer, seq, tm=FFN_UP_TILE[0], tn=FFN_UP_TILE[1]):
    m, d = h.shape
    nj = FFN_DIM // tn
    est = 2 * (tm * d * 2 + 2 * d * tn * 2 + tm * tn * 2) + nj * 2 * SUBLANES * tn * 4 + 12 * tm * tn * 4
    c_in, c_out, c_shapes, c_vmem = _cast_specs([cast_src], layer, (m // tm) * nj, lambda i, j: i * nj + j)
    return pl.pallas_call(
        functools.partial(_ffn_up_kernel, blocks_per_seq=seq // tm),
        grid=(m // tm, nj),
        in_specs=[pl.BlockSpec((tm, d), lambda i, j: (i, 0)),
                  pl.BlockSpec((None, d, tn), lambda i, j: (0, 0, j)),
                  pl.BlockSpec((None, d, tn), lambda i, j: (0, 0, nj + j)),
                  pl.BlockSpec((None, FFN_CONV, tn), lambda i, j: (layer, 0, j)),
                  pl.BlockSpec((None, FFN_CONV, tn), lambda i, j: (layer, 0, nj + j))] + c_in,
        out_specs=[pl.BlockSpec((tm, tn), lambda i, j: (i, j))] + c_out,
        out_shape=[jax.ShapeDtypeStruct((m, FFN_DIM), BF16)] + c_shapes,
        scratch_shapes=[pltpu.VMEM((nj, 2 * SUBLANES, tn), F32)],
        compiler_params=_params(("arbitrary", "arbitrary"), est + c_vmem),
        name="ffn_up_conv_gate",
    )(h, w_up, w_up, conv_w, conv_w, cast_src)


def kernel(x, norm_mix_gain, w_in, pool_w, pool_scale, fox_f_bias, gdn_conv_w, gdn_A_log, gdn_dt_bias,
           gdn_norm_gain, w_o, norm_ffn_gain, w_up, ffn_conv_w, w_down, final_norm_gain):
    batch, seq, d = x.shape
    n_layers = norm_mix_gain.shape[0]
    assert d == D_MODEL and w_in.shape[2] == IN_DIM
    assert all(seq % rows == 0 for rows in (ROW_BLOCK, FOX_BLOCK, IN_PROJ_TILE[0], OUT_PROJ_TILE[0],
                                            FFN_UP_TILE[0], FFN_DOWN_TILE[0]))
    w_big, w_small = _regroup_w_in(w_in)
    pool_w16 = pool_w.astype(BF16)
    zeros_h = jnp.zeros((n_layers, GDN_HEADS), F32)
    zeros_pad = jnp.zeros((n_layers, LANES - N_GATES), F32)
    gate_bias = jnp.concatenate([fox_f_bias.astype(F32), zeros_h, gdn_dt_bias.astype(F32), zeros_pad], axis=1)
    gate_alog = jnp.concatenate([jnp.zeros((n_layers, FOX_HEADS), F32), zeros_h, gdn_A_log.astype(F32), zeros_pad],
                                axis=1)

    xf = x.reshape(batch * seq, d).astype(F32)
    for l in range(n_layers):
        h = _rmsnorm(xf, norm_mix_gain[l], BF16)
        big, w_o16, w_up16 = _matmul(h, w_big, l, BF16, *IN_PROJ_TILE, casts=(w_o, w_up), name="in_proj")
        small = _matmul(h, w_small, l, F32, IN_PROJ_TILE[0], LANES, name="gate_proj")
        gates, gates_t = _gates(small, gate_bias[l], gate_alog[l], batch, seq)
        mix = _pool(big, pool_w16, pool_scale, l, batch, seq)
        mix = _fox_attention(big, gates, gates_t, mix, batch, seq)
        mix = _gdn(big, gates, gates_t, gdn_conv_w, gdn_norm_gain, mix, l, batch, seq)
        xf = _matmul(mix, w_o16, 0, F32, *OUT_PROJ_TILE, residual=xf, name="out_proj")
        h = _rmsnorm(xf, norm_ffn_gain[l], BF16)
        act, w_down16 = _ffn_up(h, w_up16, ffn_conv_w, w_down, l, seq)
        xf = _matmul(act, w_down16, 0, F32, *FFN_DOWN_TILE, residual=xf, name="ffn_down")
    out = _rmsnorm(xf, final_norm_gain, x.dtype)
    return out.reshape(batch, seq, d)
```

```python
import functools
import math

import jax
import jax.numpy as jnp
from jax import lax
from jax.experimental import pallas as pl
from jax.experimental.pallas import tpu as pltpu

D_MODEL = 4096
HEAD_DIM = 128
POOL_WINDOWS = (2, 4, 8, 16)
POOL_GROUPS = 4
POOL_GROUP_DIM = D_MODEL // 16
POOL_DIM = POOL_GROUPS * POOL_GROUP_DIM
ATTN_DIM = (D_MODEL - POOL_DIM) // 2
FOX_HEADS = ATTN_DIM // HEAD_DIM
GDN_DIM = D_MODEL - POOL_DIM - ATTN_DIM
GDN_HEADS = GDN_DIM // HEAD_DIM
GDN_CONV = 4
FFN_DIM = 11008
FFN_CONV = 3
EPS = 1e-6
IN_DIM = POOL_DIM + 3 * ATTN_DIM + FOX_HEADS + 3 * GDN_DIM + GDN_DIM + 2 * GDN_HEADS

LANES = 128
SUBLANES = 8
VMEM_BYTES_V7X = 64 * 1024 * 1024
VMEM_CAP = VMEM_BYTES_V7X - 8 * 1024 * 1024

ALIGNED_DIM = POOL_DIM + 3 * ATTN_DIM
BIG_DIM = ALIGNED_DIM + 3 * GDN_DIM + GDN_DIM
N_GATES = FOX_HEADS + 2 * GDN_HEADS
FOX_COL0 = POOL_DIM // LANES
GDN_COL0 = ALIGNED_DIM // LANES
Z_COL0 = (ALIGNED_DIM + 3 * GDN_DIM) // LANES
MIX_FOX_COL0 = POOL_DIM // LANES
MIX_GDN_COL0 = (POOL_DIM + ATTN_DIM) // LANES
LANE_F = 0
LANE_BETA = FOX_HEADS
LANE_G = FOX_HEADS + GDN_HEADS
GATE_ROWS = 48
GDN_CHUNK = 128

MXU_COLS = 256
ROW_BLOCK = 512
FOX_BLOCK = 512
FOX_HEADS_PER_STEP = 2
GDN_HEADS_PER_STEP = 4
IN_PROJ_TILE = (1024, 512)
OUT_PROJ_TILE = (1024, 1024)
FFN_UP_TILE = (1024, MXU_COLS)
FFN_DOWN_TILE = (512, 512)
POOL_HALO = 16
INV_BASE_BITS = 3

F32 = jnp.float32
BF16 = jnp.bfloat16
NT_DIMS = (((1,), (1,)), ((), ()))
TN_DIMS = (((0,), (0,)), ((), ()))


def _params(semantics, vmem_estimate):
    limit = min(int(vmem_estimate * 1.25) + (4 << 20), VMEM_CAP)
    return pltpu.CompilerParams(dimension_semantics=semantics, vmem_limit_bytes=limit)


def _sigmoid(x):
    return 1.0 / (1.0 + jnp.exp(-x))


def _softplus(x):
    return jnp.maximum(x, 0.0) + jnp.log1p(jnp.exp(-jnp.abs(x)))


def _shift_rows(x, prev, k):
    n, w = x.shape
    x3 = jnp.concatenate([prev, x], axis=0).reshape(n // SUBLANES + 1, SUBLANES, w)
    r = pltpu.roll(x3, k, axis=1)
    sub = lax.broadcasted_iota(jnp.int32, (n // SUBLANES, SUBLANES, w), 1)
    return jnp.where(sub < k, r[:-1], r[1:]).reshape(n, w)


REGROUP_COLS = 256


def _regroup_kernel(src_ref, big_ref):
    for l in range(src_ref.shape[1]):
        big_ref[l] = src_ref[:, l, :].T.astype(BF16)


def _regroup_w_in(w_in):
    n_layers, d, _ = w_in.shape
    assert ALIGNED_DIM % REGROUP_COLS == 0 and BIG_DIM % REGROUP_COLS == 0
    wt = jnp.transpose(w_in, (2, 0, 1))
    src_row = lambda c: c * REGROUP_COLS + jnp.where(c >= ALIGNED_DIM // REGROUP_COLS, FOX_HEADS, 0)
    est = 2 * REGROUP_COLS * n_layers * d * (4 + 2) + 4 * REGROUP_COLS * n_layers * d * 4
    big = pl.pallas_call(
        _regroup_kernel,
        grid=(BIG_DIM // REGROUP_COLS,),
        in_specs=[pl.BlockSpec((pl.Element(REGROUP_COLS), pl.Element(n_layers), pl.Element(d)),
                               lambda c: (src_row(c), 0, 0))],
        out_specs=pl.BlockSpec((n_layers, d, REGROUP_COLS), lambda c: (0, 0, c)),
        out_shape=jax.ShapeDtypeStruct((n_layers, d, BIG_DIM), BF16),
        compiler_params=_params(("arbitrary",), est),
        name="regroup_w_in",
    )(wt)
    gate_src = jnp.concatenate([wt[ALIGNED_DIM:ALIGNED_DIM + FOX_HEADS], wt[IN_DIM - 2 * GDN_HEADS:],
                                jnp.zeros((LANES - N_GATES, n_layers, d), w_in.dtype)], axis=0)
    small = pl.pallas_call(
        _regroup_kernel,
        out_shape=jax.ShapeDtypeStruct((n_layers, d, LANES), BF16),
        name="regroup_gate_cols",
    )(gate_src)
    return big, small


def _rms_kernel(x_ref, g_ref, o_ref):
    x = x_ref[...]
    ms = jnp.mean(x * x, axis=-1, keepdims=True)
    o_ref[...] = (x * lax.rsqrt(ms + EPS) * g_ref[...]).astype(o_ref.dtype)


def _rmsnorm(x, gain, out_dtype, tm=ROW_BLOCK):
    m, d = x.shape
    est = 2 * tm * d * (4 + jnp.dtype(out_dtype).itemsize)
    return pl.pallas_call(
        _rms_kernel,
        grid=(m // tm,),
        in_specs=[pl.BlockSpec((tm, d), lambda i: (i, 0)),
                  pl.BlockSpec((1, d), lambda i: (0, 0))],
        out_specs=pl.BlockSpec((tm, d), lambda i: (i, 0)),
        out_shape=jax.ShapeDtypeStruct((m, d), out_dtype),
        compiler_params=_params(("arbitrary",), est),
        name="rmsnorm",
    )(x, gain.reshape(1, d))


BF16_TILE_ROWS = 16


def _cast_specs(srcs, layer, steps, step_of):
    in_specs, out_specs, out_shapes, vmem = [], [], [], 0
    for src in srcs:
        _, rows, cols = src.shape
        rb = next(r for r in range(BF16_TILE_ROWS, rows + 1, BF16_TILE_ROWS) if rows % r == 0 and rows // r <= steps)
        blk = lambda *idx, n_blocks=rows // rb: jnp.minimum(step_of(*idx), n_blocks - 1)
        in_specs.append(pl.BlockSpec((None, rb, cols), lambda *idx, blk=blk: (layer, blk(*idx), 0)))
        out_specs.append(pl.BlockSpec((None, rb, cols), lambda *idx, blk=blk: (0, blk(*idx), 0)))
        out_shapes.append(jax.ShapeDtypeStruct((1, rows, cols), BF16))
        vmem += 2 * rb * cols * (4 + 2)
    return in_specs, out_specs, out_shapes, vmem


def _mm_kernel(*refs, has_res, n_casts):
    a_ref, b_ref = refs[0], refs[1]
    n_in = 2 + int(has_res) + n_casts
    o_ref = refs[n_in]
    acc = jnp.dot(a_ref[...], b_ref[...], preferred_element_type=F32)
    if has_res:
        acc = acc + refs[2][...]
    o_ref[...] = acc.astype(o_ref.dtype)
    for c in range(n_casts):
        refs[n_in + 1 + c][...] = refs[n_in - n_casts + c][...].astype(BF16)


def _matmul(a, w, layer, out_dtype, tm, tn, residual=None, casts=(), name="matmul"):
    m, kdim = a.shape
    n = w.shape[2]
    has_res = residual is not None
    nj = n // tn
    in_specs = [pl.BlockSpec((tm, kdim), lambda i, j: (i, 0)),
                pl.BlockSpec((None, kdim, tn), lambda i, j: (layer, 0, j))]
    args = [a, w]
    est = 2 * tm * kdim * a.dtype.itemsize + 2 * kdim * tn * w.dtype.itemsize
    if has_res:
        in_specs.append(pl.BlockSpec((tm, tn), lambda i, j: (i, j)))
        est += 2 * tm * tn * 4
        args.append(residual)
    est += 2 * tm * tn * jnp.dtype(out_dtype).itemsize + 2 * tm * tn * 4
    c_in, c_out, c_shapes, c_vmem = _cast_specs(casts, layer, (m // tm) * nj, lambda i, j: i * nj + j)
    out = pl.pallas_call(
        functools.partial(_mm_kernel, has_res=has_res, n_casts=len(casts)),
        grid=(m // tm, nj),
        in_specs=in_specs + c_in,
        out_specs=[pl.BlockSpec((tm, tn), lambda i, j: (i, j))] + c_out,
        out_shape=[jax.ShapeDtypeStruct((m, n), out_dtype)] + c_shapes,
        compiler_params=_params(("arbitrary", "arbitrary"), est + c_vmem),
        name=name,
    )(*args, *casts)
    return out[0] if not casts else tuple(out)


def _scan_rows(y, row, seg):
    pos = row & (seg - 1)
    s = 1
    while s < seg:
        y = y + jnp.where(pos >= s, pltpu.roll(y, s, axis=0), 0.0)
        s *= 2
    return y


def _gates_kernel(x_ref, p_ref, g_ref, gt_ref, carry_ref):
    t = pl.program_id(1)

    @pl.when(t == 0)
    def _():
        carry_ref[...] = jnp.zeros_like(carry_ref)

    tb = x_ref.shape[0]
    z = x_ref[...] + p_ref[0:1, :]
    lane = lax.broadcasted_iota(jnp.int32, z.shape, 1)
    row = lax.broadcasted_iota(jnp.int32, z.shape, 0)
    log_f = -_softplus(-z)
    beta = _sigmoid(z)
    g = -jnp.exp(p_ref[1:2, :]) * _softplus(z)
    cum_f = _scan_rows(log_f, row, tb) + carry_ref[0:1, :]
    carry_ref[0:1, :] = cum_f[tb - 1:tb, :]
    cum_g = _scan_rows(g, row, GDN_CHUNK)
    out = jnp.where(lane < LANE_BETA, cum_f, jnp.where(lane < LANE_G, beta, cum_g))
    g_ref[...] = out
    gt_ref[...] = out.T[:GATE_ROWS, :]


def _gates(small, bias_row, alog_row, batch, seq, tb=ROW_BLOCK):
    m = small.shape[0]
    nt = seq // tb
    params = jnp.zeros((SUBLANES, LANES), F32).at[0].set(bias_row).at[1].set(alog_row)
    est = 2 * tb * LANES * 4 * 3 + 16 * tb * LANES * 4
    return pl.pallas_call(
        _gates_kernel,
        grid=(batch, nt),
        in_specs=[pl.BlockSpec((tb, LANES), lambda b, t: (b * nt + t, 0)),
                  pl.BlockSpec((SUBLANES, LANES), lambda b, t: (0, 0))],
        out_specs=[pl.BlockSpec((tb, LANES), lambda b, t: (b * nt + t, 0)),
                   pl.BlockSpec((GATE_ROWS, tb), lambda b, t: (0, b * nt + t))],
        out_shape=[jax.ShapeDtypeStruct((m, LANES), F32),
                   jax.ShapeDtypeStruct((GATE_ROWS, m), F32)],
        scratch_shapes=[pltpu.VMEM((SUBLANES, LANES), F32)],
        compiler_params=_params(("arbitrary", "arbitrary"), est),
        name="gates",
    )(small, params)


def _pool_kernel(x_ref, w_ref, sc_ref, o_ref, tail_ref):
    t = pl.program_id(1)

    @pl.when(t == 0)
    def _():
        tail_ref[...] = jnp.zeros_like(tail_ref)

    tb = x_ref.shape[0]
    halo = tail_ref.shape[0]
    cg = POOL_GROUP_DIM
    x = x_ref[...].astype(F32)
    xe = jnp.concatenate([tail_ref[...], x], axis=0)
    tail_ref[...] = x[tb - halo:, :]
    pos = (t * tb + 1 + lax.broadcasted_iota(jnp.int32, (tb, cg), 0)).astype(F32)
    for gi, win in enumerate(POOL_WINDOWS):
        s = xe[:, gi * cg:(gi + 1) * cg]
        span = 1
        while span < win:
            s = s + pltpu.roll(s, span, axis=0)
            span *= 2
        mean = s[halo:, :] / jnp.minimum(pos, float(win))
        pooled = (mean - x[:, gi * cg:(gi + 1) * cg]).astype(BF16)
        y = jnp.dot(pooled, w_ref[gi], preferred_element_type=F32)
        o_ref[:, gi * cg:(gi + 1) * cg] = (y * sc_ref[:, gi * cg:(gi + 1) * cg]).astype(o_ref.dtype)
    o_ref[:, POOL_DIM:] = jnp.zeros((tb, o_ref.shape[1] - POOL_DIM), o_ref.dtype)


def _pool(big, pool_w, pool_scale, layer, batch, seq, tb=ROW_BLOCK):
    m = big.shape[0]
    nt = seq // tb
    halo = POOL_HALO
    assert halo >= max(POOL_WINDOWS) and halo % SUBLANES == 0
    est = 2 * tb * (POOL_DIM + D_MODEL) * 2 + 8 * tb * POOL_DIM * 4
    return pl.pallas_call(
        _pool_kernel,
        grid=(batch, nt),
        in_specs=[pl.BlockSpec((tb, POOL_DIM), lambda b, t: (b * nt + t, 0)),
                  pl.BlockSpec((None, POOL_GROUPS, POOL_GROUP_DIM, POOL_GROUP_DIM), lambda b, t: (layer, 0, 0, 0)),
                  pl.BlockSpec((None, 1, POOL_DIM), lambda b, t: (layer, 0, 0))],
        out_specs=pl.BlockSpec((tb, D_MODEL), lambda b, t: (b * nt + t, 0)),
        out_shape=jax.ShapeDtypeStruct((m, D_MODEL), BF16),
        scratch_shapes=[pltpu.VMEM((halo, POOL_DIM), F32)],
        compiler_params=_params(("arbitrary", "arbitrary"), est),
        name="pool_mixer",
    )(big, pool_w, pool_scale.reshape(pool_scale.shape[0], 1, POOL_DIM))


def _fox_kernel(q_ref, k_ref, v_ref, g_ref, cr_ref, mix_ref, o_ref, *, tq, heads):
    del mix_ref
    hg = pl.program_id(1)
    qi = pl.program_id(2)
    log2e = math.log2(math.e)
    scale2 = log2e / math.sqrt(HEAD_DIM)
    dh = HEAD_DIM
    gates = g_ref[...]
    lane = lax.broadcasted_iota(jnp.int32, gates.shape, 1)
    qs = [q_ref[:, g * dh:(g + 1) * dh] for g in range(heads)]
    cqs = [jnp.sum(jnp.where(lane == LANE_F + hg * heads + g, gates, 0.0), axis=1, keepdims=True) * log2e
           for g in range(heads)]
    tri = (lax.broadcasted_iota(jnp.int32, (tq, tq), 0) >= lax.broadcasted_iota(jnp.int32, (tq, tq), 1))

    def step(ki, carry, masked):
        start = pl.multiple_of(ki * tq, tq)
        kbs = [k_ref[pl.ds(start, tq), g * dh:(g + 1) * dh] for g in range(heads)]
        vbs = [v_ref[pl.ds(start, tq), g * dh:(g + 1) * dh] for g in range(heads)]
        ss = [lax.dot_general(qs[g], kbs[g], NT_DIMS, preferred_element_type=F32) for g in range(heads)]
        zps = [ss[g] * scale2 - cr_ref[g, ki] * log2e for g in range(heads)]
        if masked:
            zps = [jnp.where(tri, zp, -jnp.inf) for zp in zps]
        m_news = [jnp.maximum(carry[g][0], jnp.max(zps[g], axis=1, keepdims=True) + cqs[g]) for g in range(heads)]
        ps = [jnp.exp2(zps[g] - (m_news[g] - cqs[g])) for g in range(heads)]
        alphas = [jnp.exp2(carry[g][0] - m_news[g]) for g in range(heads)]
        l_news = [alphas[g] * carry[g][1] + jnp.sum(ps[g], axis=1, keepdims=True) for g in range(heads)]
        accs = [alphas[g] * carry[g][2] + jnp.dot(ps[g].astype(BF16), vbs[g], preferred_element_type=F32)
                for g in range(heads)]
        return tuple((m_news[g], l_news[g], accs[g]) for g in range(heads))

    init = tuple((jnp.full((tq, 1), -jnp.inf, F32), jnp.zeros((tq, 1), F32), jnp.zeros((tq, dh), F32))
                 for _ in range(heads))
    carry = lax.fori_loop(0, qi, lambda ki, c: step(ki, c, False), init)
    final = step(qi, carry, True)
    for g in range(heads):
        _, l_fin, acc = final[g]
        o_ref[:, g * dh:(g + 1) * dh] = (acc / l_fin).astype(o_ref.dtype)


def _fox_attention(big, gates, gates_t, mix, batch, seq, tq=FOX_BLOCK, heads=FOX_HEADS_PER_STEP):
    m = big.shape[0]
    nq = seq // tq
    width = heads * HEAD_DIM
    assert FOX_HEADS % heads == 0 and FOX_COL0 % heads == 0 and MIX_FOX_COL0 % heads == 0 and LANE_F % heads == 0
    cr = gates_t.reshape(GATE_ROWS, m // tq, 1, tq)
    est = (2 * (tq * width * 2 * 2 + 2 * seq * width * 2 + tq * LANES * 4 + heads * seq * 4 * 8)
           + heads * 10 * tq * tq * 4)
    return pl.pallas_call(
        functools.partial(_fox_kernel, tq=tq, heads=heads),
        grid=(batch, FOX_HEADS // heads, nq),
        in_specs=[
            pl.BlockSpec((tq, width), lambda b, h, i: (b * nq + i, FOX_COL0 // heads + h)),
            pl.BlockSpec((seq, width), lambda b, h, i: (b, (FOX_COL0 + FOX_HEADS) // heads + h)),
            pl.BlockSpec((seq, width), lambda b, h, i: (b, (FOX_COL0 + 2 * FOX_HEADS) // heads + h)),
            pl.BlockSpec((tq, LANES), lambda b, h, i: (b * nq + i, 0)),
            pl.BlockSpec((heads, nq, 1, tq), lambda b, h, i: (LANE_F // heads + h, b, 0, 0)),
            pl.BlockSpec(memory_space=pl.ANY),
        ],
        out_specs=pl.BlockSpec((tq, width), lambda b, h, i: (b * nq + i, MIX_FOX_COL0 // heads + h)),
        out_shape=jax.ShapeDtypeStruct(mix.shape, mix.dtype),
        input_output_aliases={5: 0},
        compiler_params=_params(("arbitrary", "arbitrary", "arbitrary"), est),
        name="fox_attention",
    )(big, big, big, gates, cr, mix)


def _bdot(a, b):
    return jnp.dot(a.astype(BF16), b.astype(BF16), preferred_element_type=F32)


def _inv_unit_lower(lows, row, col):
    n = lows[0].shape[0]
    eye = (row == col).astype(F32)
    assert INV_BASE_BITS == 3
    diag = (row >> INV_BASE_BITS) == (col >> INV_BASE_BITS)
    ds = [jnp.where(diag, low, 0.0) for low in lows]
    d2s = [_bdot(d, d) for d in ds]
    d4s = [_bdot(d2, d2) for d2 in d2s]
    invs = [eye - d for d in ds]
    invs = [inv + _bdot(inv, d2) for inv, d2 in zip(invs, d2s)]
    invs = [inv + _bdot(inv, d4) for inv, d4 in zip(invs, d4s)]
    shift = INV_BASE_BITS
    while (1 << shift) < n:
        rb = row >> shift
        cb = col >> shift
        join = ((rb & 1) == 1) & (cb == rb - 1)
        inv16s = [inv.astype(BF16) for inv in invs]
        xs = [_bdot(jnp.where(join, low, 0.0), inv16) for low, inv16 in zip(lows, inv16s)]
        invs = [inv - _bdot(inv16, x) for inv, inv16, x in zip(invs, inv16s, xs)]
        shift += 1
    return invs


def _gdn_kernel(q_ref, k_ref, v_ref, z_ref, g_ref, gr_ref, cwq_ref, cwk_ref, cwv_ref, gain_ref, mix_ref,
                o_ref, state_ref, tail_ref, *, heads):
    del mix_ref
    hg = pl.program_id(1)
    t = pl.program_id(2)

    @pl.when(t == 0)
    def _():
        state_ref[...] = jnp.zeros_like(state_ref)
        tail_ref[...] = jnp.zeros_like(tail_ref)

    tb = q_ref.shape[0]
    c = GDN_CHUNK
    dk = HEAD_DIM

    def conv_silu(x_ref, w_ref, slot):
        assert GDN_CONV == 4
        x = x_ref[...].astype(F32)
        xe = jnp.concatenate([tail_ref[slot], x], axis=0)
        x1 = pltpu.roll(xe, 1, axis=0)
        w = w_ref[...]
        near = x * w[3:4, :] + x1[SUBLANES:, :] * w[2:3, :]
        far = xe * w[1:2, :] + x1 * w[0:1, :]
        y = near + pltpu.roll(far, 2, axis=0)[SUBLANES:, :]
        tail_ref[slot] = x[tb - SUBLANES:, :]
        return y * _sigmoid(y)

    def l2n(x):
        return x * lax.rsqrt(jnp.sum(x * x, axis=-1, keepdims=True) + EPS)

    q_raw = conv_silu(q_ref, cwq_ref, 0)
    k_raw = conv_silu(k_ref, cwk_ref, 1)
    v_raw = conv_silu(v_ref, cwv_ref, 2)
    gates = g_ref[...]
    lane = lax.broadcasted_iota(jnp.int32, gates.shape, 1)

    row = lax.broadcasted_iota(jnp.int32, (c, c), 0)
    col = lax.broadcasted_iota(jnp.int32, (c, c), 1)
    causal = row >= col
    strict = row > col
    last_lane = lax.broadcasted_iota(jnp.int32, (1, c), 1) == c - 1
    chunks = [slice(ci * c, (ci + 1) * c) for ci in range(tb // c)]

    hd = []
    for g in range(heads):
        cols = slice(g * dk, (g + 1) * dk)
        head = hg * heads + g
        q = l2n(q_raw[:, cols]) * (dk ** -0.5)
        k = l2n(k_raw[:, cols])
        beta = jnp.sum(jnp.where(lane == LANE_BETA + head, gates, 0.0), axis=1, keepdims=True)
        gcol = jnp.sum(jnp.where(lane == LANE_G + head, gates, 0.0), axis=1, keepdims=True)
        exp_g = jnp.exp(gcol)
        k_beta = k * beta
        hd.append(dict(cols=cols, q16=q.astype(BF16), k=k, k16=k.astype(BF16), kb16=k_beta.astype(BF16),
                       gcol=gcol, grow=gr_ref[g],
                       rhs=jnp.concatenate([v_raw[:, cols] * beta, k_beta * exp_g], axis=1).astype(BF16),
                       q_dec=q * exp_g))

    items = [(g, sl) for sl in chunks for g in range(heads)]
    decays = [jnp.exp(jnp.where(causal, hd[g]["gcol"][sl] - hd[g]["grow"][:, sl], -jnp.inf)) for g, sl in items]
    a_mats = [jnp.where(strict, lax.dot_general(hd[g]["kb16"][sl], hd[g]["k16"][sl], NT_DIMS,
                                                preferred_element_type=F32) * dec, 0.0)
              for (g, sl), dec in zip(items, decays)]
    intras = [jnp.where(causal, lax.dot_general(hd[g]["q16"][sl], hd[g]["k16"][sl], NT_DIMS,
                                                preferred_element_type=F32) * dec, 0.0).astype(BF16)
              for (g, sl), dec in zip(items, decays)]
    invs = _inv_unit_lower(a_mats, row, col)
    sols = [jnp.dot(inv.astype(BF16), hd[g]["rhs"][sl], preferred_element_type=F32)
            for (g, sl), inv in zip(items, invs)]

    states = [state_ref[g] for g in range(heads)]
    for ci, sl in enumerate(chunks):
        base = ci * heads
        g_lasts = [jnp.sum(jnp.where(last_lane, hd[g]["grow"][:, sl], 0.0), axis=1, keepdims=True)
                   for g in range(heads)]
        k_decs = [(hd[g]["k"][sl] * jnp.exp(g_lasts[g] - hd[g]["gcol"][sl])).astype(BF16) for g in range(heads)]
        wqs = [jnp.concatenate([sols[base + g][:, dk:], hd[g]["q_dec"][sl]], axis=0).astype(BF16)
               for g in range(heads)]
        wss = [jnp.dot(wqs[g], states[g].astype(BF16), preferred_element_type=F32) for g in range(heads)]
        vns = [(sols[base + g][:, :dk] - wss[g][:c]).astype(BF16) for g in range(heads)]
        outs = [wss[g][c:] + jnp.dot(intras[base + g], vns[g], preferred_element_type=F32) for g in range(heads)]
        states = [states[g] * jnp.exp(g_lasts[g])
                  + lax.dot_general(k_decs[g], vns[g], TN_DIMS, preferred_element_type=F32) for g in range(heads)]
        for g in range(heads):
            cols = hd[g]["cols"]
            z = z_ref[sl, cols].astype(F32)
            o = outs[g]
            o = o * lax.rsqrt(jnp.mean(o * o, axis=-1, keepdims=True) + EPS) * gain_ref[...]
            o_ref[sl, cols] = (o * (z * _sigmoid(z))).astype(o_ref.dtype)
    for g in range(heads):
        state_ref[g] = states[g]


def _gdn(big, gates, gates_t, conv_w, norm_gain, mix, layer, batch, seq, tb=ROW_BLOCK, heads=GDN_HEADS_PER_STEP):
    m = big.shape[0]
    nt = seq // tb
    width = heads * HEAD_DIM
    assert GDN_HEADS % heads == 0 and all(off % heads == 0 for off in (GDN_COL0, Z_COL0, MIX_GDN_COL0, LANE_G))
    gr = gates_t.reshape(GATE_ROWS, 1, m)
    est = 2 * (5 * tb * width * 2 + tb * LANES * 4 + heads * tb * 32) + 64 * tb * width * 4
    blk = lambda off: pl.BlockSpec((tb, width), lambda b, h, t: (b * nt + t, off // heads + h))
    cw = lambda off: pl.BlockSpec((None, GDN_CONV, width), lambda b, h, t: (layer, 0, off // heads + h))
    return pl.pallas_call(
        functools.partial(_gdn_kernel, heads=heads),
        grid=(batch, GDN_HEADS // heads, nt),
        in_specs=[
            blk(GDN_COL0), blk(GDN_COL0 + GDN_HEADS), blk(GDN_COL0 + 2 * GDN_HEADS), blk(Z_COL0),
            pl.BlockSpec((tb, LANES), lambda b, h, t: (b * nt + t, 0)),
            pl.BlockSpec((heads, 1, tb), lambda b, h, t: (LANE_G // heads + h, 0, b * nt + t)),
            cw(0), cw(GDN_HEADS), cw(2 * GDN_HEADS),
            pl.BlockSpec((None, 1, HEAD_DIM), lambda b, h, t: (layer, 0, 0)),
            pl.BlockSpec(memory_space=pl.ANY),
        ],
        out_specs=pl.BlockSpec((tb, width), lambda b, h, t: (b * nt + t, MIX_GDN_COL0 // heads + h)),
        out_shape=jax.ShapeDtypeStruct(mix.shape, mix.dtype),
        input_output_aliases={10: 0},
        scratch_shapes=[pltpu.VMEM((heads, HEAD_DIM, HEAD_DIM), F32),
                        pltpu.VMEM((3, SUBLANES, width), F32)],
        compiler_params=_params(("arbitrary", "arbitrary", "arbitrary"), est),
        name="gated_delta_rule",
    )(big, big, big, big, gates, gr, conv_w, conv_w, conv_w,
      norm_gain.reshape(norm_gain.shape[0], 1, HEAD_DIM), mix)


def _ffn_up_kernel(h_ref, wg_ref, wu_ref, cg_ref, cu_ref, cast_in_ref, o_ref, cast_out_ref, tail_ref, *,
                   blocks_per_seq):
    i = pl.program_id(0)
    j = pl.program_id(1)
    tm = h_ref.shape[0]

    @pl.when(i % blocks_per_seq == 0)
    def _():
        tail_ref[j] = jnp.zeros(tail_ref.shape[1:], F32)

    h = h_ref[...]
    yg = jnp.dot(h, wg_ref[...], preferred_element_type=F32)
    yu = jnp.dot(h, wu_ref[...], preferred_element_type=F32)
    tail = tail_ref[j]

    def conv(y, prev, w):
        out = y * w[FFN_CONV - 1:FFN_CONV, :]
        for back in range(1, FFN_CONV):
            out = out + _shift_rows(y, prev, back) * w[FFN_CONV - 1 - back:FFN_CONV - back, :]
        return out

    ug = conv(yg, tail[:SUBLANES], cg_ref[...])
    uu = conv(yu, tail[SUBLANES:], cu_ref[...])
    tail_ref[j] = jnp.concatenate([yg[tm - SUBLANES:], yu[tm - SUBLANES:]], axis=0)
    o_ref[...] = (ug * _sigmoid(ug) * uu).astype(o_ref.dtype)
    cast_out_ref[...] = cast_in_ref[...].astype(BF16)


def _ffn_up(h, w_up, conv_w, cast_src, layer, seq, tm=FFN_UP_TILE[0], tn=FFN_UP_TILE[1]):
    m, d = h.shape
    nj = FFN_DIM // tn
    est = 2 * (tm * d * 2 + 2 * d * tn * 2 + tm * tn * 2) + nj * 2 * SUBLANES * tn * 4 + 12 * tm * tn * 4
    c_in, c_out, c_shapes, c_vmem = _cast_specs([cast_src], layer, (m // tm) * nj, lambda i, j: i * nj + j)
    return pl.pallas_call(
        functools.partial(_ffn_up_kernel, blocks_per_seq=seq // tm),
        grid=(m // tm, nj),
        in_specs=[pl.BlockSpec((tm, d), lambda i, j: (i, 0)),
                  pl.BlockSpec((None, d, tn), lambda i, j: (0, 0, j)),
                  pl.BlockSpec((None, d, tn), lambda i, j: (0, 0, nj + j)),
                  pl.BlockSpec((None, FFN_CONV, tn), lambda i, j: (layer, 0, j)),
                  pl.BlockSpec((None, FFN_CONV, tn), lambda i, j: (layer, 0, nj + j))] + c_in,
        out_specs=[pl.BlockSpec((tm, tn), lambda i, j: (i, j))] + c_out,
        out_shape=[jax.ShapeDtypeStruct((m, FFN_DIM), BF16)] + c_shapes,
        scratch_shapes=[pltpu.VMEM((nj, 2 * SUBLANES, tn), F32)],
        compiler_params=_params(("arbitrary", "arbitrary"), est + c_vmem),
        name="ffn_up_conv_gate",
    )(h, w_up, w_up, conv_w, conv_w, cast_src)


def kernel(x, norm_mix_gain, w_in, pool_w, pool_scale, fox_f_bias, gdn_conv_w, gdn_A_log, gdn_dt_bias,
           gdn_norm_gain, w_o, norm_ffn_gain, w_up, ffn_conv_w, w_down, final_norm_gain):
    batch, seq, d = x.shape
    n_layers = norm_mix_gain.shape[0]
    assert d == D_MODEL and w_in.shape[2] == IN_DIM
    assert all(seq % rows == 0 for rows in (ROW_BLOCK, FOX_BLOCK, IN_PROJ_TILE[0], OUT_PROJ_TILE[0],
                                            FFN_UP_TILE[0], FFN_DOWN_TILE[0]))
    w_big, w_small = _regroup_w_in(w_in)
    pool_w16 = pool_w.astype(BF16)
    zeros_h = jnp.zeros((n_layers, GDN_HEADS), F32)
    zeros_pad = jnp.zeros((n_layers, LANES - N_GATES), F32)
    gate_bias = jnp.concatenate([fox_f_bias.astype(F32), zeros_h, gdn_dt_bias.astype(F32), zeros_pad], axis=1)
    gate_alog = jnp.concatenate([jnp.zeros((n_layers, FOX_HEADS), F32), zeros_h, gdn_A_log.astype(F32), zeros_pad],
                                axis=1)

    xf = x.reshape(batch * seq, d).astype(F32)
    for l in range(n_layers):
        h = _rmsnorm(xf, norm_mix_gain[l], BF16)
        big, w_o16, w_up16 = _matmul(h, w_big, l, BF16, *IN_PROJ_TILE, casts=(w_o, w_up), name="in_proj")
        small = _matmul(h, w_small, l, F32, IN_PROJ_TILE[0], LANES, name="gate_proj")
        gates, gates_t = _gates(small, gate_bias[l], gate_alog[l], batch, seq)
        mix = _pool(big, pool_w16, pool_scale, l, batch, seq)
        mix = _fox_attention(big, gates, gates_t, mix, batch, seq)
        mix = _gdn(big, gates, gates_t, gdn_conv_w, gdn_norm_gain, mix, l, batch, seq)
        xf = _matmul(mix, w_o16, 0, F32, *OUT_PROJ_TILE, residual=xf, name="out_proj")
        h = _rmsnorm(xf, norm_ffn_gain[l], BF16)
        act, w_down16 = _ffn_up(h, w_up16, ffn_conv_w, w_down, l, seq)
        xf = _matmul(act, w_down16, 0, F32, *FFN_DOWN_TILE, residual=xf, name="ffn_down")
    out = _rmsnorm(xf, final_norm_gain, x.dtype)
    return out.reshape(batch, seq, d)
```

```python
import functools
import math

import jax
import jax.numpy as jnp
from jax import lax
from jax.experimental import pallas as pl
from jax.experimental.pallas import tpu as pltpu

D_MODEL = 4096
HEAD_DIM = 128
POOL_WINDOWS = (2, 4, 8, 16)
POOL_GROUPS = 4
POOL_GROUP_DIM = D_MODEL // 16
POOL_DIM = POOL_GROUPS * POOL_GROUP_DIM
ATTN_DIM = (D_MODEL - POOL_DIM) // 2
FOX_HEADS = ATTN_DIM // HEAD_DIM
GDN_DIM = D_MODEL - POOL_DIM - ATTN_DIM
GDN_HEADS = GDN_DIM // HEAD_DIM
GDN_CONV = 4
FFN_DIM = 11008
FFN_CONV = 3
EPS = 1e-6
IN_DIM = POOL_DIM + 3 * ATTN_DIM + FOX_HEADS + 3 * GDN_DIM + GDN_DIM + 2 * GDN_HEADS

LANES = 128
SUBLANES = 8
VMEM_BYTES_V7X = 64 * 1024 * 1024
VMEM_CAP = VMEM_BYTES_V7X - 8 * 1024 * 1024

ALIGNED_DIM = POOL_DIM + 3 * ATTN_DIM
BIG_DIM = ALIGNED_DIM + 3 * GDN_DIM + GDN_DIM
N_GATES = FOX_HEADS + 2 * GDN_HEADS
FOX_COL0 = POOL_DIM // LANES
GDN_COL0 = ALIGNED_DIM // LANES
Z_COL0 = (ALIGNED_DIM + 3 * GDN_DIM) // LANES
MIX_FOX_COL0 = POOL_DIM // LANES
MIX_GDN_COL0 = (POOL_DIM + ATTN_DIM) // LANES
LANE_F = 0
LANE_BETA = FOX_HEADS
LANE_G = FOX_HEADS + GDN_HEADS
GATE_ROWS = 48
GDN_CHUNK = 128

MXU_COLS = 256
ROW_BLOCK = 512
FOX_BLOCK = 512
FOX_HEADS_PER_STEP = 2
GDN_HEADS_PER_STEP = 4
IN_PROJ_TILE = (1024, 512)
OUT_PROJ_TILE = (1024, 1024)
FFN_UP_TILE = (1024, MXU_COLS)
ROW_GROUP = 512
FFN_DOWN_TILE = (ROW_GROUP, 512)
POOL_HALO = 16
INV_BASE_BITS = 3

F32 = jnp.float32
BF16 = jnp.bfloat16
NT_DIMS = (((1,), (1,)), ((), ()))
TN_DIMS = (((0,), (0,)), ((), ()))


def _params(semantics, vmem_estimate):
    limit = min(int(vmem_estimate * 1.25) + (4 << 20), VMEM_CAP)
    return pltpu.CompilerParams(dimension_semantics=semantics, vmem_limit_bytes=limit)


def _sigmoid(x):
    return 1.0 / (1.0 + jnp.exp(-x))


def _softplus(x):
    return jnp.maximum(x, 0.0) + jnp.log1p(jnp.exp(-jnp.abs(x)))


def _shift_rows(x, prev, k):
    n, w = x.shape
    x3 = jnp.concatenate([prev, x], axis=0).reshape(n // SUBLANES + 1, SUBLANES, w)
    r = pltpu.roll(x3, k, axis=1)
    sub = lax.broadcasted_iota(jnp.int32, (n // SUBLANES, SUBLANES, w), 1)
    return jnp.where(sub < k, r[:-1], r[1:]).reshape(n, w)


REGROUP_COLS = 256


def _regroup_kernel(src_ref, big_ref):
    for l in range(src_ref.shape[1]):
        big_ref[l] = src_ref[:, l, :].T.astype(BF16)


def _regroup_w_in(w_in):
    n_layers, d, _ = w_in.shape
    assert ALIGNED_DIM % REGROUP_COLS == 0 and BIG_DIM % REGROUP_COLS == 0
    wt = jnp.transpose(w_in, (2, 0, 1))
    src_row = lambda c: c * REGROUP_COLS + jnp.where(c >= ALIGNED_DIM // REGROUP_COLS, FOX_HEADS, 0)
    est = 2 * REGROUP_COLS * n_layers * d * (4 + 2) + 4 * REGROUP_COLS * n_layers * d * 4
    big = pl.pallas_call(
        _regroup_kernel,
        grid=(BIG_DIM // REGROUP_COLS,),
        in_specs=[pl.BlockSpec((pl.Element(REGROUP_COLS), pl.Element(n_layers), pl.Element(d)),
                               lambda c: (src_row(c), 0, 0))],
        out_specs=pl.BlockSpec((n_layers, d, REGROUP_COLS), lambda c: (0, 0, c)),
        out_shape=jax.ShapeDtypeStruct((n_layers, d, BIG_DIM), BF16),
        compiler_params=_params(("arbitrary",), est),
        name="regroup_w_in",
    )(wt)
    gate_src = jnp.concatenate([wt[ALIGNED_DIM:ALIGNED_DIM + FOX_HEADS], wt[IN_DIM - 2 * GDN_HEADS:],
                                jnp.zeros((LANES - N_GATES, n_layers, d), w_in.dtype)], axis=0)
    small = pl.pallas_call(
        _regroup_kernel,
        out_shape=jax.ShapeDtypeStruct((n_layers, d, LANES), BF16),
        name="regroup_gate_cols",
    )(gate_src)
    return big, small


def _rms_kernel(x_ref, g_ref, o_ref, *scratch, interleave):
    x = x_ref[...]
    ms = jnp.mean(x * x, axis=-1, keepdims=True)
    h = x * lax.rsqrt(ms + EPS) * g_ref[...]
    if not interleave:
        o_ref[...] = h.astype(o_ref.dtype)
        return
    h_ref, = scratch
    nv = ROW_GROUP // SUBLANES
    for c in range(h_ref.shape[0]):
        cols = slice(c * LANES, (c + 1) * LANES)
        h_ref[c] = h[:, cols]
        for g0 in range(0, x.shape[0], ROW_GROUP):
            for v in range(0, nv, 2):
                pair = [h_ref[c, pl.ds(g0 + v + dv, SUBLANES, stride=nv), :] for dv in (0, 1)]
                o_ref[g0 + SUBLANES * v:g0 + SUBLANES * (v + 2), cols] = (
                    jnp.concatenate(pair, axis=0).astype(o_ref.dtype))


def _rmsnorm(x, gain, out_dtype, tm=ROW_BLOCK, interleave=False):
    m, d = x.shape
    assert not interleave or tm % ROW_GROUP == 0
    est = 2 * tm * d * (4 + jnp.dtype(out_dtype).itemsize) + tm * d * 4
    return pl.pallas_call(
        functools.partial(_rms_kernel, interleave=interleave),
        grid=(m // tm,),
        in_specs=[pl.BlockSpec((tm, d), lambda i: (i, 0)),
                  pl.BlockSpec((1, d), lambda i: (0, 0))],
        out_specs=pl.BlockSpec((tm, d), lambda i: (i, 0)),
        out_shape=jax.ShapeDtypeStruct((m, d), out_dtype),
        scratch_shapes=[pltpu.VMEM((d // LANES, tm, LANES), F32)] if interleave else [],
        compiler_params=_params(("arbitrary",), est),
        name="rmsnorm",
    )(x, gain.reshape(1, d))


BF16_TILE_ROWS = 16


def _cast_specs(srcs, layer, steps, step_of):
    in_specs, out_specs, out_shapes, vmem = [], [], [], 0
    for src in srcs:
        _, rows, cols = src.shape
        rb = next(r for r in range(BF16_TILE_ROWS, rows + 1, BF16_TILE_ROWS) if rows % r == 0 and rows // r <= steps)
        blk = lambda *idx, n_blocks=rows // rb: jnp.minimum(step_of(*idx), n_blocks - 1)
        in_specs.append(pl.BlockSpec((None, rb, cols), lambda *idx, blk=blk: (layer, blk(*idx), 0)))
        out_specs.append(pl.BlockSpec((None, rb, cols), lambda *idx, blk=blk: (0, blk(*idx), 0)))
        out_shapes.append(jax.ShapeDtypeStruct((1, rows, cols), BF16))
        vmem += 2 * rb * cols * (4 + 2)
    return in_specs, out_specs, out_shapes, vmem


def _mm_kernel(*refs, has_res, n_casts, interleaved):
    a_ref, b_ref = refs[0], refs[1]
    n_in = 2 + int(has_res) + n_casts
    o_ref = refs[n_in]
    acc = jnp.dot(a_ref[...], b_ref[...], preferred_element_type=F32)
    if interleaved:
        acc_ref = refs[-1]
        nv = ROW_GROUP // SUBLANES
        for c in range(acc_ref.shape[0]):
            cols = slice(c * LANES, (c + 1) * LANES)
            acc_ref[c] = acc[:, cols]
            for u in range(acc.shape[0] // SUBLANES):
                rows = slice(SUBLANES * u, SUBLANES * (u + 1))
                val = acc_ref[c, pl.ds(nv * (u % SUBLANES) + u // SUBLANES, SUBLANES, stride=SUBLANES), :]
                if has_res:
                    val = val + refs[2][rows, cols]
                o_ref[rows, cols] = val.astype(o_ref.dtype)
    else:
        if has_res:
            acc = acc + refs[2][...]
        o_ref[...] = acc.astype(o_ref.dtype)
    for c in range(n_casts):
        refs[n_in + 1 + c][...] = refs[n_in - n_casts + c][...].astype(BF16)


def _matmul(a, w, layer, out_dtype, tm, tn, residual=None, casts=(), interleaved=False, name="matmul"):
    assert not interleaved or tm == ROW_GROUP
    m, kdim = a.shape
    n = w.shape[2]
    has_res = residual is not None
    nj = n // tn
    in_specs = [pl.BlockSpec((tm, kdim), lambda i, j: (i, 0)),
                pl.BlockSpec((None, kdim, tn), lambda i, j: (layer, 0, j))]
    args = [a, w]
    est = 2 * tm * kdim * a.dtype.itemsize + 2 * kdim * tn * w.dtype.itemsize
    if has_res:
        in_specs.append(pl.BlockSpec((tm, tn), lambda i, j: (i, j)))
        est += 2 * tm * tn * 4
        args.append(residual)
    est += 2 * tm * tn * jnp.dtype(out_dtype).itemsize + 2 * tm * tn * 4
    c_in, c_out, c_shapes, c_vmem = _cast_specs(casts, layer, (m // tm) * nj, lambda i, j: i * nj + j)
    out = pl.pallas_call(
        functools.partial(_mm_kernel, has_res=has_res, n_casts=len(casts), interleaved=interleaved),
        grid=(m // tm, nj),
        in_specs=in_specs + c_in,
        out_specs=[pl.BlockSpec((tm, tn), lambda i, j: (i, j))] + c_out,
        out_shape=[jax.ShapeDtypeStruct((m, n), out_dtype)] + c_shapes,
        scratch_shapes=[pltpu.VMEM((tn // LANES, tm, LANES), F32)] if interleaved else [],
        compiler_params=_params(("arbitrary", "arbitrary"), est + c_vmem),
        name=name,
    )(*args, *casts)
    return out[0] if not casts else tuple(out)


def _scan_rows(y, row, seg):
    pos = row & (seg - 1)
    s = 1
    while s < seg:
        y = y + jnp.where(pos >= s, pltpu.roll(y, s, axis=0), 0.0)
        s *= 2
    return y


def _gates_kernel(x_ref, p_ref, g_ref, gt_ref, carry_ref):
    t = pl.program_id(1)

    @pl.when(t == 0)
    def _():
        carry_ref[...] = jnp.zeros_like(carry_ref)

    tb = x_ref.shape[0]
    z = x_ref[...] + p_ref[0:1, :]
    lane = lax.broadcasted_iota(jnp.int32, z.shape, 1)
    row = lax.broadcasted_iota(jnp.int32, z.shape, 0)
    log_f = -_softplus(-z)
    beta = _sigmoid(z)
    g = -jnp.exp(p_ref[1:2, :]) * _softplus(z)
    cum_f = _scan_rows(log_f, row, tb) + carry_ref[0:1, :]
    carry_ref[0:1, :] = cum_f[tb - 1:tb, :]
    cum_g = _scan_rows(g, row, GDN_CHUNK)
    out = jnp.where(lane < LANE_BETA, cum_f, jnp.where(lane < LANE_G, beta, cum_g))
    g_ref[...] = out
    gt_ref[...] = out.T[:GATE_ROWS, :]


def _gates(small, bias_row, alog_row, batch, seq, tb=ROW_BLOCK):
    m = small.shape[0]
    nt = seq // tb
    params = jnp.zeros((SUBLANES, LANES), F32).at[0].set(bias_row).at[1].set(alog_row)
    est = 2 * tb * LANES * 4 * 3 + 16 * tb * LANES * 4
    return pl.pallas_call(
        _gates_kernel,
        grid=(batch, nt),
        in_specs=[pl.BlockSpec((tb, LANES), lambda b, t: (b * nt + t, 0)),
                  pl.BlockSpec((SUBLANES, LANES), lambda b, t: (0, 0))],
        out_specs=[pl.BlockSpec((tb, LANES), lambda b, t: (b * nt + t, 0)),
                   pl.BlockSpec((GATE_ROWS, tb), lambda b, t: (0, b * nt + t))],
        out_shape=[jax.ShapeDtypeStruct((m, LANES), F32),
                   jax.ShapeDtypeStruct((GATE_ROWS, m), F32)],
        scratch_shapes=[pltpu.VMEM((SUBLANES, LANES), F32)],
        compiler_params=_params(("arbitrary", "arbitrary"), est),
        name="gates",
    )(small, params)


def _pool_kernel(x_ref, w_ref, sc_ref, o_ref, tail_ref):
    t = pl.program_id(1)

    @pl.when(t == 0)
    def _():
        tail_ref[...] = jnp.zeros_like(tail_ref)

    tb = x_ref.shape[0]
    halo = tail_ref.shape[0]
    cg = POOL_GROUP_DIM
    x = x_ref[...].astype(F32)
    xe = jnp.concatenate([tail_ref[...], x], axis=0)
    tail_ref[...] = x[tb - halo:, :]
    pos = (t * tb + 1 + lax.broadcasted_iota(jnp.int32, (tb, cg), 0)).astype(F32)
    for gi, win in enumerate(POOL_WINDOWS):
        s = xe[:, gi * cg:(gi + 1) * cg]
        span = 1
        while span < win:
            s = s + pltpu.roll(s, span, axis=0)
            span *= 2
        mean = s[halo:, :] / jnp.minimum(pos, float(win))
        pooled = (mean - x[:, gi * cg:(gi + 1) * cg]).astype(BF16)
        y = jnp.dot(pooled, w_ref[gi], preferred_element_type=F32)
        o_ref[:, gi * cg:(gi + 1) * cg] = (y * sc_ref[:, gi * cg:(gi + 1) * cg]).astype(o_ref.dtype)
    o_ref[:, POOL_DIM:] = jnp.zeros((tb, o_ref.shape[1] - POOL_DIM), o_ref.dtype)


def _pool(big, pool_w, pool_scale, layer, batch, seq, tb=ROW_BLOCK):
    m = big.shape[0]
    nt = seq // tb
    halo = POOL_HALO
    assert halo >= max(POOL_WINDOWS) and halo % SUBLANES == 0
    est = 2 * tb * (POOL_DIM + D_MODEL) * 2 + 8 * tb * POOL_DIM * 4
    return pl.pallas_call(
        _pool_kernel,
        grid=(batch, nt),
        in_specs=[pl.BlockSpec((tb, POOL_DIM), lambda b, t: (b * nt + t, 0)),
                  pl.BlockSpec((None, POOL_GROUPS, POOL_GROUP_DIM, POOL_GROUP_DIM), lambda b, t: (layer, 0, 0, 0)),
                  pl.BlockSpec((None, 1, POOL_DIM), lambda b, t: (layer, 0, 0))],
        out_specs=pl.BlockSpec((tb, D_MODEL), lambda b, t: (b * nt + t, 0)),
        out_shape=jax.ShapeDtypeStruct((m, D_MODEL), BF16),
        scratch_shapes=[pltpu.VMEM((halo, POOL_DIM), F32)],
        compiler_params=_params(("arbitrary", "arbitrary"), est),
        name="pool_mixer",
    )(big, pool_w, pool_scale.reshape(pool_scale.shape[0], 1, POOL_DIM))


def _fox_kernel(q_ref, k_ref, v_ref, g_ref, cr_ref, mix_ref, o_ref, *, tq, heads):
    del mix_ref
    hg = pl.program_id(1)
    qi = pl.program_id(2)
    log2e = math.log2(math.e)
    scale2 = log2e / math.sqrt(HEAD_DIM)
    dh = HEAD_DIM
    gates = g_ref[...]
    lane = lax.broadcasted_iota(jnp.int32, gates.shape, 1)
    qs = [q_ref[:, g * dh:(g + 1) * dh] for g in range(heads)]
    cqs = [jnp.sum(jnp.where(lane == LANE_F + hg * heads + g, gates, 0.0), axis=1, keepdims=True) * log2e
           for g in range(heads)]
    tri = (lax.broadcasted_iota(jnp.int32, (tq, tq), 0) >= lax.broadcasted_iota(jnp.int32, (tq, tq), 1))

    def step(ki, carry, masked):
        start = pl.multiple_of(ki * tq, tq)
        kbs = [k_ref[pl.ds(start, tq), g * dh:(g + 1) * dh] for g in range(heads)]
        vbs = [v_ref[pl.ds(start, tq), g * dh:(g + 1) * dh] for g in range(heads)]
        ss = [lax.dot_general(qs[g], kbs[g], NT_DIMS, preferred_element_type=F32) for g in range(heads)]
        zps = [ss[g] * scale2 - cr_ref[g, ki] * log2e for g in range(heads)]
        if masked:
            zps = [jnp.where(tri, zp, -jnp.inf) for zp in zps]
        m_news = [jnp.maximum(carry[g][0], jnp.max(zps[g], axis=1, keepdims=True) + cqs[g]) for g in range(heads)]
        ps = [jnp.exp2(zps[g] - (m_news[g] - cqs[g])) for g in range(heads)]
        alphas = [jnp.exp2(carry[g][0] - m_news[g]) for g in range(heads)]
        l_news = [alphas[g] * carry[g][1] + jnp.sum(ps[g], axis=1, keepdims=True) for g in range(heads)]
        accs = [alphas[g] * carry[g][2] + jnp.dot(ps[g].astype(BF16), vbs[g], preferred_element_type=F32)
                for g in range(heads)]
        return tuple((m_news[g], l_news[g], accs[g]) for g in range(heads))

    init = tuple((jnp.full((tq, 1), -jnp.inf, F32), jnp.zeros((tq, 1), F32), jnp.zeros((tq, dh), F32))
                 for _ in range(heads))
    carry = lax.fori_loop(0, qi, lambda ki, c: step(ki, c, False), init)
    final = step(qi, carry, True)
    for g in range(heads):
        _, l_fin, acc = final[g]
        o_ref[:, g * dh:(g + 1) * dh] = (acc / l_fin).astype(o_ref.dtype)


def _fox_attention(big, gates, gates_t, mix, batch, seq, tq=FOX_BLOCK, heads=FOX_HEADS_PER_STEP):
    m = big.shape[0]
    nq = seq // tq
    width = heads * HEAD_DIM
    assert FOX_HEADS % heads == 0 and FOX_COL0 % heads == 0 and MIX_FOX_COL0 % heads == 0 and LANE_F % heads == 0
    cr = gates_t.reshape(GATE_ROWS, m // tq, 1, tq)
    est = (2 * (tq * width * 2 * 2 + 2 * seq * width * 2 + tq * LANES * 4 + heads * seq * 4 * 8)
           + heads * 10 * tq * tq * 4)
    return pl.pallas_call(
        functools.partial(_fox_kernel, tq=tq, heads=heads),
        grid=(batch, FOX_HEADS // heads, nq),
        in_specs=[
            pl.BlockSpec((tq, width), lambda b, h, i: (b * nq + i, FOX_COL0 // heads + h)),
            pl.BlockSpec((seq, width), lambda b, h, i: (b, (FOX_COL0 + FOX_HEADS) // heads + h)),
            pl.BlockSpec((seq, width), lambda b, h, i: (b, (FOX_COL0 + 2 * FOX_HEADS) // heads + h)),
            pl.BlockSpec((tq, LANES), lambda b, h, i: (b * nq + i, 0)),
            pl.BlockSpec((heads, nq, 1, tq), lambda b, h, i: (LANE_F // heads + h, b, 0, 0)),
            pl.BlockSpec(memory_space=pl.ANY),
        ],
        out_specs=pl.BlockSpec((tq, width), lambda b, h, i: (b * nq + i, MIX_FOX_COL0 // heads + h)),
        out_shape=jax.ShapeDtypeStruct(mix.shape, mix.dtype),
        input_output_aliases={5: 0},
        compiler_params=_params(("arbitrary", "arbitrary", "arbitrary"), est),
        name="fox_attention",
    )(big, big, big, gates, cr, mix)


def _bdot(a, b):
    return jnp.dot(a.astype(BF16), b.astype(BF16), preferred_element_type=F32)


def _inv_unit_lower(lows, row, col):
    n = lows[0].shape[0]
    eye = (row == col).astype(F32)
    assert INV_BASE_BITS == 3
    diag = (row >> INV_BASE_BITS) == (col >> INV_BASE_BITS)
    ds = [jnp.where(diag, low, 0.0) for low in lows]
    d2s = [_bdot(d, d) for d in ds]
    d4s = [_bdot(d2, d2) for d2 in d2s]
    invs = [eye - d for d in ds]
    invs = [inv + _bdot(inv, d2) for inv, d2 in zip(invs, d2s)]
    invs = [inv + _bdot(inv, d4) for inv, d4 in zip(invs, d4s)]
    shift = INV_BASE_BITS
    while (1 << shift) < n:
        rb = row >> shift
        cb = col >> shift
        join = ((rb & 1) == 1) & (cb == rb - 1)
        inv16s = [inv.astype(BF16) for inv in invs]
        xs = [_bdot(jnp.where(join, low, 0.0), inv16) for low, inv16 in zip(lows, inv16s)]
        invs = [inv - _bdot(inv16, x) for inv, inv16, x in zip(invs, inv16s, xs)]
        shift += 1
    return invs


def _gdn_kernel(q_ref, k_ref, v_ref, z_ref, g_ref, gr_ref, cwq_ref, cwk_ref, cwv_ref, gain_ref, mix_ref,
                o_ref, state_ref, tail_ref, *, heads):
    del mix_ref
    hg = pl.program_id(1)
    t = pl.program_id(2)

    @pl.when(t == 0)
    def _():
        state_ref[...] = jnp.zeros_like(state_ref)
        tail_ref[...] = jnp.zeros_like(tail_ref)

    tb = q_ref.shape[0]
    c = GDN_CHUNK
    dk = HEAD_DIM

    def conv_silu(x_ref, w_ref, slot):
        assert GDN_CONV == 4
        x = x_ref[...].astype(F32)
        xe = jnp.concatenate([tail_ref[slot], x], axis=0)
        x1 = pltpu.roll(xe, 1, axis=0)
        w = w_ref[...]
        near = x * w[3:4, :] + x1[SUBLANES:, :] * w[2:3, :]
        far = xe * w[1:2, :] + x1 * w[0:1, :]
        y = near + pltpu.roll(far, 2, axis=0)[SUBLANES:, :]
        tail_ref[slot] = x[tb - SUBLANES:, :]
        return y * _sigmoid(y)

    def l2n(x):
        return x * lax.rsqrt(jnp.sum(x * x, axis=-1, keepdims=True) + EPS)

    q_raw = conv_silu(q_ref, cwq_ref, 0)
    k_raw = conv_silu(k_ref, cwk_ref, 1)
    v_raw = conv_silu(v_ref, cwv_ref, 2)
    gates = g_ref[...]
    lane = lax.broadcasted_iota(jnp.int32, gates.shape, 1)

    row = lax.broadcasted_iota(jnp.int32, (c, c), 0)
    col = lax.broadcasted_iota(jnp.int32, (c, c), 1)
    causal = row >= col
    strict = row > col
    last_lane = lax.broadcasted_iota(jnp.int32, (1, c), 1) == c - 1
    chunks = [slice(ci * c, (ci + 1) * c) for ci in range(tb // c)]

    hd = []
    for g in range(heads):
        cols = slice(g * dk, (g + 1) * dk)
        head = hg * heads + g
        q = l2n(q_raw[:, cols]) * (dk ** -0.5)
        k = l2n(k_raw[:, cols])
        beta = jnp.sum(jnp.where(lane == LANE_BETA + head, gates, 0.0), axis=1, keepdims=True)
        gcol = jnp.sum(jnp.where(lane == LANE_G + head, gates, 0.0), axis=1, keepdims=True)
        exp_g = jnp.exp(gcol)
        k_beta = k * beta
        hd.append(dict(cols=cols, q16=q.astype(BF16), k=k, k16=k.astype(BF16), kb16=k_beta.astype(BF16),
                       gcol=gcol, grow=gr_ref[g],
                       rhs=jnp.concatenate([v_raw[:, cols] * beta, k_beta * exp_g], axis=1).astype(BF16),
                       q_dec=q * exp_g))

    items = [(g, sl) for sl in chunks for g in range(heads)]
    decays = [jnp.exp(jnp.where(causal, hd[g]["gcol"][sl] - hd[g]["grow"][:, sl], -jnp.inf)) for g, sl in items]
    a_mats = [jnp.where(strict, lax.dot_general(hd[g]["kb16"][sl], hd[g]["k16"][sl], NT_DIMS,
                                                preferred_element_type=F32) * dec, 0.0)
              for (g, sl), dec in zip(items, decays)]
    intras = [jnp.where(causal, lax.dot_general(hd[g]["q16"][sl], hd[g]["k16"][sl], NT_DIMS,
                                                preferred_element_type=F32) * dec, 0.0).astype(BF16)
              for (g, sl), dec in zip(items, decays)]
    invs = _inv_unit_lower(a_mats, row, col)
    sols = [jnp.dot(inv.astype(BF16), hd[g]["rhs"][sl], preferred_element_type=F32)
            for (g, sl), inv in zip(items, invs)]

    states = [state_ref[g] for g in range(heads)]
    for ci, sl in enumerate(chunks):
        base = ci * heads
        g_lasts = [jnp.sum(jnp.where(last_lane, hd[g]["grow"][:, sl], 0.0), axis=1, keepdims=True)
                   for g in range(heads)]
        k_decs = [(hd[g]["k"][sl] * jnp.exp(g_lasts[g] - hd[g]["gcol"][sl])).astype(BF16) for g in range(heads)]
        wqs = [jnp.concatenate([sols[base + g][:, dk:], hd[g]["q_dec"][sl]], axis=0).astype(BF16)
               for g in range(heads)]
        wss = [jnp.dot(wqs[g], states[g].astype(BF16), preferred_element_type=F32) for g in range(heads)]
        vns = [(sols[base + g][:, :dk] - wss[g][:c]).astype(BF16) for g in range(heads)]
        outs = [wss[g][c:] + jnp.dot(intras[base + g], vns[g], preferred_element_type=F32) for g in range(heads)]
        states = [states[g] * jnp.exp(g_lasts[g])
                  + lax.dot_general(k_decs[g], vns[g], TN_DIMS, preferred_element_type=F32) for g in range(heads)]
        for g in range(heads):
            cols = hd[g]["cols"]
            z = z_ref[sl, cols].astype(F32)
            o = outs[g]
            o = o * lax.rsqrt(jnp.mean(o * o, axis=-1, keepdims=True) + EPS) * gain_ref[...]
            o_ref[sl, cols] = (o * (z * _sigmoid(z))).astype(o_ref.dtype)
    for g in range(heads):
        state_ref[g] = states[g]


def _gdn(big, gates, gates_t, conv_w, norm_gain, mix, layer, batch, seq, tb=ROW_BLOCK, heads=GDN_HEADS_PER_STEP):
    m = big.shape[0]
    nt = seq // tb
    width = heads * HEAD_DIM
    assert GDN_HEADS % heads == 0 and all(off % heads == 0 for off in (GDN_COL0, Z_COL0, MIX_GDN_COL0, LANE_G))
    gr = gates_t.reshape(GATE_ROWS, 1, m)
    est = 2 * (5 * tb * width * 2 + tb * LANES * 4 + heads * tb * 32) + 64 * tb * width * 4
    blk = lambda off: pl.BlockSpec((tb, width), lambda b, h, t: (b * nt + t, off // heads + h))
    cw = lambda off: pl.BlockSpec((None, GDN_CONV, width), lambda b, h, t: (layer, 0, off // heads + h))
    return pl.pallas_call(
        functools.partial(_gdn_kernel, heads=heads),
        grid=(batch, GDN_HEADS // heads, nt),
        in_specs=[
            blk(GDN_COL0), blk(GDN_COL0 + GDN_HEADS), blk(GDN_COL0 + 2 * GDN_HEADS), blk(Z_COL0),
            pl.BlockSpec((tb, LANES), lambda b, h, t: (b * nt + t, 0)),
            pl.BlockSpec((heads, 1, tb), lambda b, h, t: (LANE_G // heads + h, 0, b * nt + t)),
            cw(0), cw(GDN_HEADS), cw(2 * GDN_HEADS),
            pl.BlockSpec((None, 1, HEAD_DIM), lambda b, h, t: (layer, 0, 0)),
            pl.BlockSpec(memory_space=pl.ANY),
        ],
        out_specs=pl.BlockSpec((tb, width), lambda b, h, t: (b * nt + t, MIX_GDN_COL0 // heads + h)),
        out_shape=jax.ShapeDtypeStruct(mix.shape, mix.dtype),
        input_output_aliases={10: 0},
        scratch_shapes=[pltpu.VMEM((heads, HEAD_DIM, HEAD_DIM), F32),
                        pltpu.VMEM((3, SUBLANES, width), F32)],
        compiler_params=_params(("arbitrary", "arbitrary", "arbitrary"), est),
        name="gated_delta_rule",
    )(big, big, big, big, gates, gr, conv_w, conv_w, conv_w,
      norm_gain.reshape(norm_gain.shape[0], 1, HEAD_DIM), mix)


def _ffn_up_kernel(h_ref, wg_ref, wu_ref, cg_ref, cu_ref, cast_in_ref, o_ref, cast_out_ref, tail_ref, *,
                   blocks_per_seq):
    i = pl.program_id(0)
    j = pl.program_id(1)
    tm = h_ref.shape[0]

    @pl.when(i % blocks_per_seq == 0)
    def _():
        tail_ref[j] = jnp.zeros(tail_ref.shape[1:], F32)

    h = h_ref[...]
    yg = jnp.dot(h, wg_ref[...], preferred_element_type=F32)
    yu = jnp.dot(h, wu_ref[...], preferred_element_type=F32)
    tail = tail_ref[j]

    nv = ROW_GROUP // SUBLANES
    keep = FFN_CONV - 1
    sub0 = lax.broadcasted_iota(jnp.int32, (keep, SUBLANES, yg.shape[1]), 1) == 0

    def conv(y, prev3, w):
        outs = []
        for g0 in range(0, tm, ROW_GROUP):
            y3 = y[g0:g0 + ROW_GROUP].reshape(nv, SUBLANES, y.shape[1])
            wrapped = jnp.where(sub0, pltpu.roll(prev3, 1, axis=1), pltpu.roll(y3[nv - keep:], 1, axis=1))
            out = y3 * w[FFN_CONV - 1:FFN_CONV, :]
            for back in range(1, FFN_CONV):
                shifted = jnp.concatenate([wrapped[keep - back:], y3[:nv - back]], axis=0)
                out = out + shifted * w[FFN_CONV - 1 - back:FFN_CONV - back, :]
            outs.append(out.reshape(ROW_GROUP, y.shape[1]))
            prev3 = y3[nv - keep:]
        return jnp.concatenate(outs, axis=0), prev3

    tail3 = tail.reshape(2, keep, SUBLANES, yg.shape[1])
    ug, last_g = conv(yg, tail3[0], cg_ref[...])
    uu, last_u = conv(yu, tail3[1], cu_ref[...])
    tail_ref[j] = jnp.concatenate([last_g, last_u], axis=0).reshape(tail.shape)
    o_ref[...] = (ug * _sigmoid(ug) * uu).astype(o_ref.dtype)
    cast_out_ref[...] = cast_in_ref[...].astype(BF16)


def _ffn_up(h, w_up, conv_w, cast_src, layer, seq, tm=FFN_UP_TILE[0], tn=FFN_UP_TILE[1]):
    m, d = h.shape
    nj = FFN_DIM // tn
    est = 2 * (tm * d * 2 + 2 * d * tn * 2 + tm * tn * 2) + nj * 2 * SUBLANES * tn * 4 + 12 * tm * tn * 4
    c_in, c_out, c_shapes, c_vmem = _cast_specs([cast_src], layer, (m // tm) * nj, lambda i, j: i * nj + j)
    return pl.pallas_call(
        functools.partial(_ffn_up_kernel, blocks_per_seq=seq // tm),
        grid=(m // tm, nj),
        in_specs=[pl.BlockSpec((tm, d), lambda i, j: (i, 0)),
                  pl.BlockSpec((None, d, tn), lambda i, j: (0, 0, j)),
                  pl.BlockSpec((None, d, tn), lambda i, j: (0, 0, nj + j)),
                  pl.BlockSpec((None, FFN_CONV, tn), lambda i, j: (layer, 0, j)),
                  pl.BlockSpec((None, FFN_CONV, tn), lambda i, j: (layer, 0, nj + j))] + c_in,
        out_specs=[pl.BlockSpec((tm, tn), lambda i, j: (i, j))] + c_out,
        out_shape=[jax.ShapeDtypeStruct((m, FFN_DIM), BF16)] + c_shapes,
        scratch_shapes=[pltpu.VMEM((nj, 2 * (FFN_CONV - 1) * SUBLANES, tn), F32)],
        compiler_params=_params(("arbitrary", "arbitrary"), est + c_vmem),
        name="ffn_up_conv_gate",
    )(h, w_up, w_up, conv_w, conv_w, cast_src)


def kernel(x, norm_mix_gain, w_in, pool_w, pool_scale, fox_f_bias, gdn_conv_w, gdn_A_log, gdn_dt_bias,
           gdn_norm_gain, w_o, norm_ffn_gain, w_up, ffn_conv_w, w_down, final_norm_gain):
    batch, seq, d = x.shape
    n_layers = norm_mix_gain.shape[0]
    assert d == D_MODEL and w_in.shape[2] == IN_DIM
    assert all(seq % rows == 0 for rows in (ROW_BLOCK, FOX_BLOCK, IN_PROJ_TILE[0], OUT_PROJ_TILE[0],
                                            FFN_UP_TILE[0], FFN_DOWN_TILE[0]))
    w_big, w_small = _regroup_w_in(w_in)
    pool_w16 = pool_w.astype(BF16)
    zeros_h = jnp.zeros((n_layers, GDN_HEADS), F32)
    zeros_pad = jnp.zeros((n_layers, LANES - N_GATES), F32)
    gate_bias = jnp.concatenate([fox_f_bias.astype(F32), zeros_h, gdn_dt_bias.astype(F32), zeros_pad], axis=1)
    gate_alog = jnp.concatenate([jnp.zeros((n_layers, FOX_HEADS), F32), zeros_h, gdn_A_log.astype(F32), zeros_pad],
                                axis=1)

    xf = x.reshape(batch * seq, d).astype(F32)
    for l in range(n_layers):
        h = _rmsnorm(xf, norm_mix_gain[l], BF16)
        big, w_o16, w_up16 = _matmul(h, w_big, l, BF16, *IN_PROJ_TILE, casts=(w_o, w_up), name="in_proj")
        small = _matmul(h, w_small, l, F32, IN_PROJ_TILE[0], LANES, name="gate_proj")
        gates, gates_t = _gates(small, gate_bias[l], gate_alog[l], batch, seq)
        mix = _pool(big, pool_w16, pool_scale, l, batch, seq)
        mix = _fox_attention(big, gates, gates_t, mix, batch, seq)
        mix = _gdn(big, gates, gates_t, gdn_conv_w, gdn_norm_gain, mix, l, batch, seq)
        xf = _matmul(mix, w_o16, 0, F32, *OUT_PROJ_TILE, residual=xf, name="out_proj")
        h = _rmsnorm(xf, norm_ffn_gain[l], BF16, interleave=True)
        act, w_down16 = _ffn_up(h, w_up16, ffn_conv_w, w_down, l, seq)
        xf = _matmul(act, w_down16, 0, F32, *FFN_DOWN_TILE, residual=xf, interleaved=True, name="ffn_down")
    out = _rmsnorm(xf, final_norm_gain, x.dtype)
    return out.reshape(batch, seq, d)
```

```python
import functools
import math

import jax
import jax.numpy as jnp
from jax import lax
from jax.experimental import pallas as pl
from jax.experimental.pallas import tpu as pltpu

D_MODEL = 4096
HEAD_DIM = 128
POOL_WINDOWS = (2, 4, 8, 16)
POOL_GROUPS = 4
POOL_GROUP_DIM = D_MODEL // 16
POOL_DIM = POOL_GROUPS * POOL_GROUP_DIM
ATTN_DIM = (D_MODEL - POOL_DIM) // 2
FOX_HEADS = ATTN_DIM // HEAD_DIM
GDN_DIM = D_MODEL - POOL_DIM - ATTN_DIM
GDN_HEADS = GDN_DIM // HEAD_DIM
GDN_CONV = 4
FFN_DIM = 11008
FFN_CONV = 3
EPS = 1e-6
IN_DIM = POOL_DIM + 3 * ATTN_DIM + FOX_HEADS + 3 * GDN_DIM + GDN_DIM + 2 * GDN_HEADS

LANES = 128
SUBLANES = 8
VMEM_BYTES_V7X = 64 * 1024 * 1024
VMEM_CAP = VMEM_BYTES_V7X - 8 * 1024 * 1024

ALIGNED_DIM = POOL_DIM + 3 * ATTN_DIM
BIG_DIM = ALIGNED_DIM + 3 * GDN_DIM + GDN_DIM
N_GATES = FOX_HEADS + 2 * GDN_HEADS
FOX_COL0 = POOL_DIM // LANES
GDN_COL0 = ALIGNED_DIM // LANES
Z_COL0 = (ALIGNED_DIM + 3 * GDN_DIM) // LANES
MIX_FOX_COL0 = POOL_DIM // LANES
MIX_GDN_COL0 = (POOL_DIM + ATTN_DIM) // LANES
LANE_F = 0
LANE_BETA = FOX_HEADS
LANE_G = FOX_HEADS + GDN_HEADS
GATE_ROWS = 48
GDN_CHUNK = 128

MXU_COLS = 256
ROW_BLOCK = 512
FOX_BLOCK = 512
FOX_HEADS_PER_STEP = 2
GDN_HEADS_PER_STEP = 4
IN_PROJ_TILE = (1024, 512)
OUT_PROJ_TILE = (1024, 1024)
FFN_UP_TILE = (1024, MXU_COLS)
ROW_GROUP = 512
FFN_DOWN_TILE = (ROW_GROUP, 512)
POOL_HALO = 16
INV_BASE_BITS = 3

F32 = jnp.float32
BF16 = jnp.bfloat16
NT_DIMS = (((1,), (1,)), ((), ()))
TN_DIMS = (((0,), (0,)), ((), ()))


def _params(semantics, vmem_estimate):
    limit = min(int(vmem_estimate * 1.25) + (4 << 20), VMEM_CAP)
    return pltpu.CompilerParams(dimension_semantics=semantics, vmem_limit_bytes=limit)


def _sigmoid(x):
    return 1.0 / (1.0 + jnp.exp(-x))


def _softplus(x):
    return jnp.maximum(x, 0.0) + jnp.log1p(jnp.exp(-jnp.abs(x)))


def _shift_rows(x, prev, k):
    n, w = x.shape
    x3 = jnp.concatenate([prev, x], axis=0).reshape(n // SUBLANES + 1, SUBLANES, w)
    r = pltpu.roll(x3, k, axis=1)
    sub = lax.broadcasted_iota(jnp.int32, (n // SUBLANES, SUBLANES, w), 1)
    return jnp.where(sub < k, r[:-1], r[1:]).reshape(n, w)


REGROUP_COLS = 256


def _regroup_kernel(src_ref, big_ref):
    for l in range(src_ref.shape[1]):
        big_ref[l] = src_ref[:, l, :].T.astype(BF16)


def _regroup_w_in(w_in):
    n_layers, d, _ = w_in.shape
    assert ALIGNED_DIM % REGROUP_COLS == 0 and BIG_DIM % REGROUP_COLS == 0
    wt = jnp.transpose(w_in, (2, 0, 1))
    src_row = lambda c: c * REGROUP_COLS + jnp.where(c >= ALIGNED_DIM // REGROUP_COLS, FOX_HEADS, 0)
    est = 2 * REGROUP_COLS * n_layers * d * (4 + 2) + 4 * REGROUP_COLS * n_layers * d * 4
    big = pl.pallas_call(
        _regroup_kernel,
        grid=(BIG_DIM // REGROUP_COLS,),
        in_specs=[pl.BlockSpec((pl.Element(REGROUP_COLS), pl.Element(n_layers), pl.Element(d)),
                               lambda c: (src_row(c), 0, 0))],
        out_specs=pl.BlockSpec((n_layers, d, REGROUP_COLS), lambda c: (0, 0, c)),
        out_shape=jax.ShapeDtypeStruct((n_layers, d, BIG_DIM), BF16),
        compiler_params=_params(("arbitrary",), est),
        name="regroup_w_in",
    )(wt)
    gate_src = jnp.concatenate([wt[ALIGNED_DIM:ALIGNED_DIM + FOX_HEADS], wt[IN_DIM - 2 * GDN_HEADS:],
                                jnp.zeros((LANES - N_GATES, n_layers, d), w_in.dtype)], axis=0)
    small = pl.pallas_call(
        _regroup_kernel,
        out_shape=jax.ShapeDtypeStruct((n_layers, d, LANES), BF16),
        name="regroup_gate_cols",
    )(gate_src)
    return big, small


def _rms_kernel(x_ref, g_ref, o_ref, *, interleave):
    x = x_ref[...]
    ms = jnp.mean(x * x, axis=-1, keepdims=True)
    h = x * lax.rsqrt(ms + EPS) * g_ref[...]
    if not interleave:
        o_ref[...] = h.astype(o_ref.dtype)
        return
    nv = ROW_GROUP // SUBLANES
    for g0 in range(0, x.shape[0], ROW_GROUP):
        group = h[g0:g0 + ROW_GROUP].reshape(SUBLANES, nv, h.shape[1])
        o_ref[g0:g0 + ROW_GROUP, :] = jnp.swapaxes(group, 0, 1).reshape(ROW_GROUP, h.shape[1]).astype(o_ref.dtype)


def _rmsnorm(x, gain, out_dtype, tm=ROW_BLOCK, interleave=False):
    m, d = x.shape
    assert not interleave or tm % ROW_GROUP == 0
    est = 2 * tm * d * (4 + jnp.dtype(out_dtype).itemsize) + 2 * tm * d * 4
    return pl.pallas_call(
        functools.partial(_rms_kernel, interleave=interleave),
        grid=(m // tm,),
        in_specs=[pl.BlockSpec((tm, d), lambda i: (i, 0)),
                  pl.BlockSpec((1, d), lambda i: (0, 0))],
        out_specs=pl.BlockSpec((tm, d), lambda i: (i, 0)),
        out_shape=jax.ShapeDtypeStruct((m, d), out_dtype),
        compiler_params=_params(("arbitrary",), est),
        name="rmsnorm",
    )(x, gain.reshape(1, d))


BF16_TILE_ROWS = 16


def _cast_specs(srcs, layer, steps, step_of):
    in_specs, out_specs, out_shapes, vmem = [], [], [], 0
    for src in srcs:
        _, rows, cols = src.shape
        rb = next(r for r in range(BF16_TILE_ROWS, rows + 1, BF16_TILE_ROWS) if rows % r == 0 and rows // r <= steps)
        blk = lambda *idx, n_blocks=rows // rb: jnp.minimum(step_of(*idx), n_blocks - 1)
        in_specs.append(pl.BlockSpec((None, rb, cols), lambda *idx, blk=blk: (layer, blk(*idx), 0)))
        out_specs.append(pl.BlockSpec((None, rb, cols), lambda *idx, blk=blk: (0, blk(*idx), 0)))
        out_shapes.append(jax.ShapeDtypeStruct((1, rows, cols), BF16))
        vmem += 2 * rb * cols * (4 + 2)
    return in_specs, out_specs, out_shapes, vmem


def _mm_kernel(*refs, has_res, n_casts, interleaved):
    a_ref, b_ref = refs[0], refs[1]
    n_in = 2 + int(has_res) + n_casts
    o_ref = refs[n_in]
    acc = jnp.dot(a_ref[...], b_ref[...], preferred_element_type=F32)
    if interleaved:
        acc_ref = refs[-1]
        nv = ROW_GROUP // SUBLANES
        for c in range(acc_ref.shape[0]):
            cols = slice(c * LANES, (c + 1) * LANES)
            acc_ref[c] = acc[:, cols]
            for u in range(acc.shape[0] // SUBLANES):
                rows = slice(SUBLANES * u, SUBLANES * (u + 1))
                val = acc_ref[c, pl.ds(nv * (u % SUBLANES) + u // SUBLANES, SUBLANES, stride=SUBLANES), :]
                if has_res:
                    val = val + refs[2][rows, cols]
                o_ref[rows, cols] = val.astype(o_ref.dtype)
    else:
        if has_res:
            acc = acc + refs[2][...]
        o_ref[...] = acc.astype(o_ref.dtype)
    for c in range(n_casts):
        refs[n_in + 1 + c][...] = refs[n_in - n_casts + c][...].astype(BF16)


def _matmul(a, w, layer, out_dtype, tm, tn, residual=None, casts=(), interleaved=False, name="matmul"):
    assert not interleaved or tm == ROW_GROUP
    m, kdim = a.shape
    n = w.shape[2]
    has_res = residual is not None
    nj = n // tn
    in_specs = [pl.BlockSpec((tm, kdim), lambda i, j: (i, 0)),
                pl.BlockSpec((None, kdim, tn), lambda i, j: (layer, 0, j))]
    args = [a, w]
    est = 2 * tm * kdim * a.dtype.itemsize + 2 * kdim * tn * w.dtype.itemsize
    if has_res:
        in_specs.append(pl.BlockSpec((tm, tn), lambda i, j: (i, j)))
        est += 2 * tm * tn * 4
        args.append(residual)
    est += 2 * tm * tn * jnp.dtype(out_dtype).itemsize + 2 * tm * tn * 4
    c_in, c_out, c_shapes, c_vmem = _cast_specs(casts, layer, (m // tm) * nj, lambda i, j: i * nj + j)
    out = pl.pallas_call(
        functools.partial(_mm_kernel, has_res=has_res, n_casts=len(casts), interleaved=interleaved),
        grid=(m // tm, nj),
        in_specs=in_specs + c_in,
        out_specs=[pl.BlockSpec((tm, tn), lambda i, j: (i, j))] + c_out,
        out_shape=[jax.ShapeDtypeStruct((m, n), out_dtype)] + c_shapes,
        scratch_shapes=[pltpu.VMEM((tn // LANES, tm, LANES), F32)] if interleaved else [],
        compiler_params=_params(("arbitrary", "arbitrary"), est + c_vmem),
        name=name,
    )(*args, *casts)
    return out[0] if not casts else tuple(out)


def _scan_rows(y, row, seg):
    pos = row & (seg - 1)
    s = 1
    while s < seg:
        y = y + jnp.where(pos >= s, pltpu.roll(y, s, axis=0), 0.0)
        s *= 2
    return y


def _gates_kernel(x_ref, p_ref, g_ref, gt_ref, carry_ref):
    t = pl.program_id(1)

    @pl.when(t == 0)
    def _():
        carry_ref[...] = jnp.zeros_like(carry_ref)

    tb = x_ref.shape[0]
    z = x_ref[...] + p_ref[0:1, :]
    lane = lax.broadcasted_iota(jnp.int32, z.shape, 1)
    row = lax.broadcasted_iota(jnp.int32, z.shape, 0)
    log_f = -_softplus(-z)
    beta = _sigmoid(z)
    g = -jnp.exp(p_ref[1:2, :]) * _softplus(z)
    cum_f = _scan_rows(log_f, row, tb) + carry_ref[0:1, :]
    carry_ref[0:1, :] = cum_f[tb - 1:tb, :]
    cum_g = _scan_rows(g, row, GDN_CHUNK)
    out = jnp.where(lane < LANE_BETA, cum_f, jnp.where(lane < LANE_G, beta, cum_g))
    g_ref[...] = out
    gt_ref[...] = out.T[:GATE_ROWS, :]


def _gates(small, bias_row, alog_row, batch, seq, tb=ROW_BLOCK):
    m = small.shape[0]
    nt = seq // tb
    params = jnp.zeros((SUBLANES, LANES), F32).at[0].set(bias_row).at[1].set(alog_row)
    est = 2 * tb * LANES * 4 * 3 + 16 * tb * LANES * 4
    return pl.pallas_call(
        _gates_kernel,
        grid=(batch, nt),
        in_specs=[pl.BlockSpec((tb, LANES), lambda b, t: (b * nt + t, 0)),
                  pl.BlockSpec((SUBLANES, LANES), lambda b, t: (0, 0))],
        out_specs=[pl.BlockSpec((tb, LANES), lambda b, t: (b * nt + t, 0)),
                   pl.BlockSpec((GATE_ROWS, tb), lambda b, t: (0, b * nt + t))],
        out_shape=[jax.ShapeDtypeStruct((m, LANES), F32),
                   jax.ShapeDtypeStruct((GATE_ROWS, m), F32)],
        scratch_shapes=[pltpu.VMEM((SUBLANES, LANES), F32)],
        compiler_params=_params(("arbitrary", "arbitrary"), est),
        name="gates",
    )(small, params)


def _pool_kernel(x_ref, w_ref, sc_ref, o_ref, tail_ref):
    t = pl.program_id(1)

    @pl.when(t == 0)
    def _():
        tail_ref[...] = jnp.zeros_like(tail_ref)

    tb = x_ref.shape[0]
    halo = tail_ref.shape[0]
    cg = POOL_GROUP_DIM
    x = x_ref[...].astype(F32)
    xe = jnp.concatenate([tail_ref[...], x], axis=0)
    tail_ref[...] = x[tb - halo:, :]
    pos = (t * tb + 1 + lax.broadcasted_iota(jnp.int32, (tb, cg), 0)).astype(F32)
    for gi, win in enumerate(POOL_WINDOWS):
        s = xe[:, gi * cg:(gi + 1) * cg]
        span = 1
        while span < win:
            s = s + pltpu.roll(s, span, axis=0)
            span *= 2
        mean = s[halo:, :] / jnp.minimum(pos, float(win))
        pooled = (mean - x[:, gi * cg:(gi + 1) * cg]).astype(BF16)
        y = jnp.dot(pooled, w_ref[gi], preferred_element_type=F32)
        o_ref[:, gi * cg:(gi + 1) * cg] = (y * sc_ref[:, gi * cg:(gi + 1) * cg]).astype(o_ref.dtype)
    o_ref[:, POOL_DIM:] = jnp.zeros((tb, o_ref.shape[1] - POOL_DIM), o_ref.dtype)


def _pool(big, pool_w, pool_scale, layer, batch, seq, tb=ROW_BLOCK):
    m = big.shape[0]
    nt = seq // tb
    halo = POOL_HALO
    assert halo >= max(POOL_WINDOWS) and halo % SUBLANES == 0
    est = 2 * tb * (POOL_DIM + D_MODEL) * 2 + 8 * tb * POOL_DIM * 4
    return pl.pallas_call(
        _pool_kernel,
        grid=(batch, nt),
        in_specs=[pl.BlockSpec((tb, POOL_DIM), lambda b, t: (b * nt + t, 0)),
                  pl.BlockSpec((None, POOL_GROUPS, POOL_GROUP_DIM, POOL_GROUP_DIM), lambda b, t: (layer, 0, 0, 0)),
                  pl.BlockSpec((None, 1, POOL_DIM), lambda b, t: (layer, 0, 0))],
        out_specs=pl.BlockSpec((tb, D_MODEL), lambda b, t: (b * nt + t, 0)),
        out_shape=jax.ShapeDtypeStruct((m, D_MODEL), BF16),
        scratch_shapes=[pltpu.VMEM((halo, POOL_DIM), F32)],
        compiler_params=_params(("arbitrary", "arbitrary"), est),
        name="pool_mixer",
    )(big, pool_w, pool_scale.reshape(pool_scale.shape[0], 1, POOL_DIM))


def _fox_kernel(q_ref, k_ref, v_ref, g_ref, cr_ref, mix_ref, o_ref, *, tq, heads):
    del mix_ref
    hg = pl.program_id(1)
    qi = pl.program_id(2)
    log2e = math.log2(math.e)
    scale2 = log2e / math.sqrt(HEAD_DIM)
    dh = HEAD_DIM
    gates = g_ref[...]
    lane = lax.broadcasted_iota(jnp.int32, gates.shape, 1)
    qs = [q_ref[:, g * dh:(g + 1) * dh] for g in range(heads)]
    cqs = [jnp.sum(jnp.where(lane == LANE_F + hg * heads + g, gates, 0.0), axis=1, keepdims=True) * log2e
           for g in range(heads)]
    tri = (lax.broadcasted_iota(jnp.int32, (tq, tq), 0) >= lax.broadcasted_iota(jnp.int32, (tq, tq), 1))

    def step(ki, carry, masked):
        start = pl.multiple_of(ki * tq, tq)
        kbs = [k_ref[pl.ds(start, tq), g * dh:(g + 1) * dh] for g in range(heads)]
        vbs = [v_ref[pl.ds(start, tq), g * dh:(g + 1) * dh] for g in range(heads)]
        ss = [lax.dot_general(qs[g], kbs[g], NT_DIMS, preferred_element_type=F32) for g in range(heads)]
        zps = [ss[g] * scale2 - cr_ref[g, ki] * log2e for g in range(heads)]
        if masked:
            zps = [jnp.where(tri, zp, -jnp.inf) for zp in zps]
        m_news = [jnp.maximum(carry[g][0], jnp.max(zps[g], axis=1, keepdims=True) + cqs[g]) for g in range(heads)]
        ps = [jnp.exp2(zps[g] - (m_news[g] - cqs[g])) for g in range(heads)]
        alphas = [jnp.exp2(carry[g][0] - m_news[g]) for g in range(heads)]
        l_news = [alphas[g] * carry[g][1] + jnp.sum(ps[g], axis=1, keepdims=True) for g in range(heads)]
        accs = [alphas[g] * carry[g][2] + jnp.dot(ps[g].astype(BF16), vbs[g], preferred_element_type=F32)
                for g in range(heads)]
        return tuple((m_news[g], l_news[g], accs[g]) for g in range(heads))

    init = tuple((jnp.full((tq, 1), -jnp.inf, F32), jnp.zeros((tq, 1), F32), jnp.zeros((tq, dh), F32))
                 for _ in range(heads))
    carry = lax.fori_loop(0, qi, lambda ki, c: step(ki, c, False), init)
    final = step(qi, carry, True)
    for g in range(heads):
        _, l_fin, acc = final[g]
        o_ref[:, g * dh:(g + 1) * dh] = (acc / l_fin).astype(o_ref.dtype)


def _fox_attention(big, gates, gates_t, mix, batch, seq, tq=FOX_BLOCK, heads=FOX_HEADS_PER_STEP):
    m = big.shape[0]
    nq = seq // tq
    width = heads * HEAD_DIM
    assert FOX_HEADS % heads == 0 and FOX_COL0 % heads == 0 and MIX_FOX_COL0 % heads == 0 and LANE_F % heads == 0
    cr = gates_t.reshape(GATE_ROWS, m // tq, 1, tq)
    est = (2 * (tq * width * 2 * 2 + 2 * seq * width * 2 + tq * LANES * 4 + heads * seq * 4 * 8)
           + heads * 10 * tq * tq * 4)
    return pl.pallas_call(
        functools.partial(_fox_kernel, tq=tq, heads=heads),
        grid=(batch, FOX_HEADS // heads, nq),
        in_specs=[
            pl.BlockSpec((tq, width), lambda b, h, i: (b * nq + i, FOX_COL0 // heads + h)),
            pl.BlockSpec((seq, width), lambda b, h, i: (b, (FOX_COL0 + FOX_HEADS) // heads + h)),
            pl.BlockSpec((seq, width), lambda b, h, i: (b, (FOX_COL0 + 2 * FOX_HEADS) // heads + h)),
            pl.BlockSpec((tq, LANES), lambda b, h, i: (b * nq + i, 0)),
            pl.BlockSpec((heads, nq, 1, tq), lambda b, h, i: (LANE_F // heads + h, b, 0, 0)),
            pl.BlockSpec(memory_space=pl.ANY),
        ],
        out_specs=pl.BlockSpec((tq, width), lambda b, h, i: (b * nq + i, MIX_FOX_COL0 // heads + h)),
        out_shape=jax.ShapeDtypeStruct(mix.shape, mix.dtype),
        input_output_aliases={5: 0},
        compiler_params=_params(("arbitrary", "arbitrary", "arbitrary"), est),
        name="fox_attention",
    )(big, big, big, gates, cr, mix)


def _bdot(a, b):
    return jnp.dot(a.astype(BF16), b.astype(BF16), preferred_element_type=F32)


def _inv_unit_lower(lows, row, col):
    n = lows[0].shape[0]
    eye = (row == col).astype(F32)
    assert INV_BASE_BITS == 3
    diag = (row >> INV_BASE_BITS) == (col >> INV_BASE_BITS)
    ds = [jnp.where(diag, low, 0.0) for low in lows]
    d2s = [_bdot(d, d) for d in ds]
    d4s = [_bdot(d2, d2) for d2 in d2s]
    invs = [eye - d for d in ds]
    invs = [inv + _bdot(inv, d2) for inv, d2 in zip(invs, d2s)]
    invs = [inv + _bdot(inv, d4) for inv, d4 in zip(invs, d4s)]
    shift = INV_BASE_BITS
    while (1 << shift) < n:
        rb = row >> shift
        cb = col >> shift
        join = ((rb & 1) == 1) & (cb == rb - 1)
        inv16s = [inv.astype(BF16) for inv in invs]
        xs = [_bdot(jnp.where(join, low, 0.0), inv16) for low, inv16 in zip(lows, inv16s)]
        invs = [inv - _bdot(inv16, x) for inv, inv16, x in zip(invs, inv16s, xs)]
        shift += 1
    return invs


def _gdn_kernel(q_ref, k_ref, v_ref, z_ref, g_ref, gr_ref, cwq_ref, cwk_ref, cwv_ref, gain_ref, mix_ref,
                o_ref, state_ref, tail_ref, *, heads):
    del mix_ref
    hg = pl.program_id(1)
    t = pl.program_id(2)

    @pl.when(t == 0)
    def _():
        state_ref[...] = jnp.zeros_like(state_ref)
        tail_ref[...] = jnp.zeros_like(tail_ref)

    tb = q_ref.shape[0]
    c = GDN_CHUNK
    dk = HEAD_DIM

    def conv_silu(x_ref, w_ref, slot):
        assert GDN_CONV == 4
        x = x_ref[...].astype(F32)
        xe = jnp.concatenate([tail_ref[slot], x], axis=0)
        x1 = pltpu.roll(xe, 1, axis=0)
        w = w_ref[...]
        near = x * w[3:4, :] + x1[SUBLANES:, :] * w[2:3, :]
        far = xe * w[1:2, :] + x1 * w[0:1, :]
        y = near + pltpu.roll(far, 2, axis=0)[SUBLANES:, :]
        tail_ref[slot] = x[tb - SUBLANES:, :]
        return y * _sigmoid(y)

    def l2n(x):
        return x * lax.rsqrt(jnp.sum(x * x, axis=-1, keepdims=True) + EPS)

    q_raw = conv_silu(q_ref, cwq_ref, 0)
    k_raw = conv_silu(k_ref, cwk_ref, 1)
    v_raw = conv_silu(v_ref, cwv_ref, 2)
    gates = g_ref[...]
    lane = lax.broadcasted_iota(jnp.int32, gates.shape, 1)

    row = lax.broadcasted_iota(jnp.int32, (c, c), 0)
    col = lax.broadcasted_iota(jnp.int32, (c, c), 1)
    causal = row >= col
    strict = row > col
    last_lane = lax.broadcasted_iota(jnp.int32, (1, c), 1) == c - 1
    chunks = [slice(ci * c, (ci + 1) * c) for ci in range(tb // c)]

    hd = []
    for g in range(heads):
        cols = slice(g * dk, (g + 1) * dk)
        head = hg * heads + g
        q = l2n(q_raw[:, cols]) * (dk ** -0.5)
        k = l2n(k_raw[:, cols])
        beta = jnp.sum(jnp.where(lane == LANE_BETA + head, gates, 0.0), axis=1, keepdims=True)
        gcol = jnp.sum(jnp.where(lane == LANE_G + head, gates, 0.0), axis=1, keepdims=True)
        exp_g = jnp.exp(gcol)
        k_beta = k * beta
        hd.append(dict(cols=cols, q16=q.astype(BF16), k=k, k16=k.astype(BF16), kb16=k_beta.astype(BF16),
                       gcol=gcol, grow=gr_ref[g],
                       rhs=jnp.concatenate([v_raw[:, cols] * beta, k_beta * exp_g], axis=1).astype(BF16),
                       q_dec=q * exp_g))

    items = [(g, sl) for sl in chunks for g in range(heads)]
    decays = [jnp.exp(jnp.where(causal, hd[g]["gcol"][sl] - hd[g]["grow"][:, sl], -jnp.inf)) for g, sl in items]
    a_mats = [jnp.where(strict, lax.dot_general(hd[g]["kb16"][sl], hd[g]["k16"][sl], NT_DIMS,
                                                preferred_element_type=F32) * dec, 0.0)
              for (g, sl), dec in zip(items, decays)]
    intras = [jnp.where(causal, lax.dot_general(hd[g]["q16"][sl], hd[g]["k16"][sl], NT_DIMS,
                                                preferred_element_type=F32) * dec, 0.0).astype(BF16)
              for (g, sl), dec in zip(items, decays)]
    invs = _inv_unit_lower(a_mats, row, col)
    sols = [jnp.dot(inv.astype(BF16), hd[g]["rhs"][sl], preferred_element_type=F32)
            for (g, sl), inv in zip(items, invs)]

    states = [state_ref[g] for g in range(heads)]
    for ci, sl in enumerate(chunks):
        base = ci * heads
        g_lasts = [jnp.sum(jnp.where(last_lane, hd[g]["grow"][:, sl], 0.0), axis=1, keepdims=True)
                   for g in range(heads)]
        k_decs = [(hd[g]["k"][sl] * jnp.exp(g_lasts[g] - hd[g]["gcol"][sl])).astype(BF16) for g in range(heads)]
        wqs = [jnp.concatenate([sols[base + g][:, dk:], hd[g]["q_dec"][sl]], axis=0).astype(BF16)
               for g in range(heads)]
        wss = [jnp.dot(wqs[g], states[g].astype(BF16), preferred_element_type=F32) for g in range(heads)]
        vns = [(sols[base + g][:, :dk] - wss[g][:c]).astype(BF16) for g in range(heads)]
        outs = [wss[g][c:] + jnp.dot(intras[base + g], vns[g], preferred_element_type=F32) for g in range(heads)]
        states = [states[g] * jnp.exp(g_lasts[g])
                  + lax.dot_general(k_decs[g], vns[g], TN_DIMS, preferred_element_type=F32) for g in range(heads)]
        for g in range(heads):
            cols = hd[g]["cols"]
            z = z_ref[sl, cols].astype(F32)
            o = outs[g]
            o = o * lax.rsqrt(jnp.mean(o * o, axis=-1, keepdims=True) + EPS) * gain_ref[...]
            o_ref[sl, cols] = (o * (z * _sigmoid(z))).astype(o_ref.dtype)
    for g in range(heads):
        state_ref[g] = states[g]


def _gdn(big, gates, gates_t, conv_w, norm_gain, mix, layer, batch, seq, tb=ROW_BLOCK, heads=GDN_HEADS_PER_STEP):
    m = big.shape[0]
    nt = seq // tb
    width = heads * HEAD_DIM
    assert GDN_HEADS % heads == 0 and all(off % heads == 0 for off in (GDN_COL0, Z_COL0, MIX_GDN_COL0, LANE_G))
    gr = gates_t.reshape(GATE_ROWS, 1, m)
    est = 2 * (5 * tb * width * 2 + tb * LANES * 4 + heads * tb * 32) + 64 * tb * width * 4
    blk = lambda off: pl.BlockSpec((tb, width), lambda b, h, t: (b * nt + t, off // heads + h))
    cw = lambda off: pl.BlockSpec((None, GDN_CONV, width), lambda b, h, t: (layer, 0, off // heads + h))
    return pl.pallas_call(
        functools.partial(_gdn_kernel, heads=heads),
        grid=(batch, GDN_HEADS // heads, nt),
        in_specs=[
            blk(GDN_COL0), blk(GDN_COL0 + GDN_HEADS), blk(GDN_COL0 + 2 * GDN_HEADS), blk(Z_COL0),
            pl.BlockSpec((tb, LANES), lambda b, h, t: (b * nt + t, 0)),
            pl.BlockSpec((heads, 1, tb), lambda b, h, t: (LANE_G // heads + h, 0, b * nt + t)),
            cw(0), cw(GDN_HEADS), cw(2 * GDN_HEADS),
            pl.BlockSpec((None, 1, HEAD_DIM), lambda b, h, t: (layer, 0, 0)),
            pl.BlockSpec(memory_space=pl.ANY),
        ],
        out_specs=pl.BlockSpec((tb, width), lambda b, h, t: (b * nt + t, MIX_GDN_COL0 // heads + h)),
        out_shape=jax.ShapeDtypeStruct(mix.shape, mix.dtype),
        input_output_aliases={10: 0},
        scratch_shapes=[pltpu.VMEM((heads, HEAD_DIM, HEAD_DIM), F32),
                        pltpu.VMEM((3, SUBLANES, width), F32)],
        compiler_params=_params(("arbitrary", "arbitrary", "arbitrary"), est),
        name="gated_delta_rule",
    )(big, big, big, big, gates, gr, conv_w, conv_w, conv_w,
      norm_gain.reshape(norm_gain.shape[0], 1, HEAD_DIM), mix)


def _ffn_up_kernel(h_ref, wg_ref, wu_ref, cg_ref, cu_ref, cast_in_ref, o_ref, cast_out_ref, tail_ref, *,
                   blocks_per_seq):
    i = pl.program_id(0)
    j = pl.program_id(1)
    tm = h_ref.shape[0]

    @pl.when(i % blocks_per_seq == 0)
    def _():
        tail_ref[j] = jnp.zeros(tail_ref.shape[1:], F32)

    h = h_ref[...]
    yg = jnp.dot(h, wg_ref[...], preferred_element_type=F32)
    yu = jnp.dot(h, wu_ref[...], preferred_element_type=F32)
    tail = tail_ref[j]

    nv = ROW_GROUP // SUBLANES
    keep = FFN_CONV - 1
    sub0 = lax.broadcasted_iota(jnp.int32, (keep, SUBLANES, yg.shape[1]), 1) == 0

    def conv(y, prev3, w):
        outs = []
        for g0 in range(0, tm, ROW_GROUP):
            y3 = y[g0:g0 + ROW_GROUP].reshape(nv, SUBLANES, y.shape[1])
            wrapped = jnp.where(sub0, pltpu.roll(prev3, 1, axis=1), pltpu.roll(y3[nv - keep:], 1, axis=1))
            out = y3 * w[FFN_CONV - 1:FFN_CONV, :]
            for back in range(1, FFN_CONV):
                shifted = jnp.concatenate([wrapped[keep - back:], y3[:nv - back]], axis=0)
                out = out + shifted * w[FFN_CONV - 1 - back:FFN_CONV - back, :]
            outs.append(out.reshape(ROW_GROUP, y.shape[1]))
            prev3 = y3[nv - keep:]
        return jnp.concatenate(outs, axis=0), prev3

    tail3 = tail.reshape(2, keep, SUBLANES, yg.shape[1])
    ug, last_g = conv(yg, tail3[0], cg_ref[...])
    uu, last_u = conv(yu, tail3[1], cu_ref[...])
    tail_ref[j] = jnp.concatenate([last_g, last_u], axis=0).reshape(tail.shape)
    o_ref[...] = (ug * _sigmoid(ug) * uu).astype(o_ref.dtype)
    cast_out_ref[...] = cast_in_ref[...].astype(BF16)


def _ffn_up(h, w_up, conv_w, cast_src, layer, seq, tm=FFN_UP_TILE[0], tn=FFN_UP_TILE[1]):
    m, d = h.shape
    nj = FFN_DIM // tn
    est = 2 * (tm * d * 2 + 2 * d * tn * 2 + tm * tn * 2) + nj * 2 * SUBLANES * tn * 4 + 12 * tm * tn * 4
    c_in, c_out, c_shapes, c_vmem = _cast_specs([cast_src], layer, (m // tm) * nj, lambda i, j: i * nj + j)
    return pl.pallas_call(
        functools.partial(_ffn_up_kernel, blocks_per_seq=seq // tm),
        grid=(m // tm, nj),
        in_specs=[pl.BlockSpec((tm, d), lambda i, j: (i, 0)),
                  pl.BlockSpec((None, d, tn), lambda i, j: (0, 0, j)),
                  pl.BlockSpec((None, d, tn), lambda i, j: (0, 0, nj + j)),
                  pl.BlockSpec((None, FFN_CONV, tn), lambda i, j: (layer, 0, j)),
                  pl.BlockSpec((None, FFN_CONV, tn), lambda i, j: (layer, 0, nj + j))] + c_in,
        out_specs=[pl.BlockSpec((tm, tn), lambda i, j: (i, j))] + c_out,
        out_shape=[jax.ShapeDtypeStruct((m, FFN_DIM), BF16)] + c_shapes,
        scratch_shapes=[pltpu.VMEM((nj, 2 * (FFN_CONV - 1) * SUBLANES, tn), F32)],
        compiler_params=_params(("arbitrary", "arbitrary"), est + c_vmem),
        name="ffn_up_conv_gate",
    )(h, w_up, w_up, conv_w, conv_w, cast_src)


def kernel(x, norm_mix_gain, w_in, pool_w, pool_scale, fox_f_bias, gdn_conv_w, gdn_A_log, gdn_dt_bias,
           gdn_norm_gain, w_o, norm_ffn_gain, w_up, ffn_conv_w, w_down, final_norm_gain):
    batch, seq, d = x.shape
    n_layers = norm_mix_gain.shape[0]
    assert d == D_MODEL and w_in.shape[2] == IN_DIM
    assert all(seq % rows == 0 for rows in (ROW_BLOCK, FOX_BLOCK, IN_PROJ_TILE[0], OUT_PROJ_TILE[0],
                                            FFN_UP_TILE[0], FFN_DOWN_TILE[0]))
    w_big, w_small = _regroup_w_in(w_in)
    pool_w16 = pool_w.astype(BF16)
    zeros_h = jnp.zeros((n_layers, GDN_HEADS), F32)
    zeros_pad = jnp.zeros((n_layers, LANES - N_GATES), F32)
    gate_bias = jnp.concatenate([fox_f_bias.astype(F32), zeros_h, gdn_dt_bias.astype(F32), zeros_pad], axis=1)
    gate_alog = jnp.concatenate([jnp.zeros((n_layers, FOX_HEADS), F32), zeros_h, gdn_A_log.astype(F32), zeros_pad],
                                axis=1)

    xf = x.reshape(batch * seq, d).astype(F32)
    for l in range(n_layers):
        h = _rmsnorm(xf, norm_mix_gain[l], BF16)
        big, w_o16, w_up16 = _matmul(h, w_big, l, BF16, *IN_PROJ_TILE, casts=(w_o, w_up), name="in_proj")
        small = _matmul(h, w_small, l, F32, IN_PROJ_TILE[0], LANES, name="gate_proj")
        gates, gates_t = _gates(small, gate_bias[l], gate_alog[l], batch, seq)
        mix = _pool(big, pool_w16, pool_scale, l, batch, seq)
        mix = _fox_attention(big, gates, gates_t, mix, batch, seq)
        mix = _gdn(big, gates, gates_t, gdn_conv_w, gdn_norm_gain, mix, l, batch, seq)
        xf = _matmul(mix, w_o16, 0, F32, *OUT_PROJ_TILE, residual=xf, name="out_proj")
        h = _rmsnorm(xf, norm_ffn_gain[l], BF16, interleave=True)
        act, w_down16 = _ffn_up(h, w_up16, ffn_conv_w, w_down, l, seq)
        xf = _matmul(act, w_down16, 0, F32, *FFN_DOWN_TILE, residual=xf, interleaved=True, name="ffn_down")
    out = _rmsnorm(xf, final_norm_gain, x.dtype)
    return out.reshape(batch, seq, d)
```
